```python
import jax, jax.numpy as jnp
from jax import lax
import numpy as np

D_MODEL = 2048
BATCH = 2
SEQ = 4096
DEPTH = 2
DEC_BATCH = 32
DEC_SEQ = 4
PAST_LEN = 8192
PAGE_SIZE = 128

H_A = 16
DH_A = 64
D_A = H_A * DH_A
Q_BLOCK = 128
SB_BIAS_INIT = -6.0
D_B = 1024
CONV_W = 31
H_C = 8
DK_C = 128
DV_C = 128
D_C = H_C * DK_C
D_CV = H_C * DV_C
HGRN_CHUNK = 64
N_BRANCH = 3
D_FF = 5632
FFN_CONV_W = 3
EPS = 1e-6
SPLIT_SIZES = (D_A, D_A, D_A, 2 * D_B, D_C, D_CV, D_C, D_CV, N_BRANCH * D_MODEL)
D_IN = sum(SPLIT_SIZES)

kernel_name = 'hybrid_sb_conformer_hgrn2_decoder_step'


def split_offsets():
    return [int(v) for v in np.cumsum(SPLIT_SIZES)[:-1]]


def rmsnorm(x, g):
    xf = x.astype(jnp.float32)
    r = lax.rsqrt(jnp.mean(xf * xf, axis=-1, keepdims=True) + EPS)
    return (xf * r).astype(x.dtype) * g


def layernorm(x, g, b):
    xf = x.astype(jnp.float32)
    mu = jnp.mean(xf, axis=-1, keepdims=True)
    var = jnp.mean(jnp.square(xf - mu), axis=-1, keepdims=True)
    return ((xf - mu) * lax.rsqrt(var + EPS)).astype(x.dtype) * g + b


def causal_depthwise_conv(u, buf, w):
    W, C = w.shape
    ext = jnp.concatenate([buf.astype(u.dtype), u], axis=1)
    y = lax.conv_general_dilated(ext, w[:, None, :].astype(u.dtype), window_strides=(1,),
                                 padding='VALID', dimension_numbers=('NWC', 'WIO', 'NWC'),
                                 feature_group_count=C)
    return y, ext[:, ext.shape[1] - (W - 1):]


def stick_breaking_attention(q, q_pos, k_segs, v_segs, bias):
    B, T, H, Dh = q.shape
    seg_lens = [k.shape[1] for k in k_segs]
    k_pos = jnp.arange(sum(seg_lens))
    qb = min(Q_BLOCK, T)
    nb = -(-T // qb)
    pad = nb * qb - T
    q = jnp.pad(q, ((0, 0), (0, pad), (0, 0), (0, 0)))
    q_pos = jnp.pad(q_pos, (0, pad), mode='edge')
    q_blocks = q.reshape(B, nb, qb, H, Dh).transpose(1, 0, 2, 3, 4)
    pos_blocks = q_pos.reshape(nb, qb)
    scale = Dh ** -0.5
    bias_f = bias.astype(jnp.float32)[None, :, None, None]

    def block(args):
        qblk, pblk = args
        z = jnp.concatenate([jnp.einsum('bqhd,bshd->bhqs', qblk, k) for k in k_segs],
                            axis=-1).astype(jnp.float32) * scale + bias_f
        mask = k_pos[None, :] < pblk[:, None]
        log_rem = jnp.where(mask, jax.nn.log_sigmoid(-z), 0.0)
        suffix = lax.cumsum(log_rem, axis=3, reverse=True) - log_rem
        w = jnp.where(mask, jnp.exp(jax.nn.log_sigmoid(z) + suffix), 0.0)
        out = 0.0
        off = 0
        for v, n in zip(v_segs, seg_lens):
            out = out + jnp.einsum('bhqs,bshd->bqhd', w[..., off:off + n].astype(v.dtype), v)
            off += n
        return out

    o = lax.map(block, (q_blocks, pos_blocks))
    return o.transpose(1, 0, 2, 3, 4).reshape(B, nb * qb, H, Dh)[:, :T]


def hgrn2_chunked(q, k, v, log_f, s0):
    B, T, H, DK = q.shape
    DV = v.shape[-1]
    c = min(HGRN_CHUNK, T)
    nc = -(-T // c)
    pad = nc * c - T

    def chunks(a):
        a = jnp.pad(a.astype(jnp.float32), ((0, 0), (0, pad), (0, 0), (0, 0)))
        return a.reshape(B, nc, c, H, a.shape[-1]).transpose(1, 0, 3, 2, 4)

    causal = jnp.tril(jnp.ones((c, c), dtype=bool))[:, :, None]

    def step(S, inp):
        qc, kc, vc, lf = inp
        G = jnp.cumsum(lf, axis=2)
        decay = jnp.exp(jnp.where(causal, G[:, :, :, None, :] - G[:, :, None, :, :], -jnp.inf))
        A = jnp.einsum('bhtd,bhsd,bhtsd->bhts', qc, kc, decay)
        o = (jnp.einsum('bhts,bhsv->bhtv', A, vc)
             + jnp.einsum('bhtd,bhdv->bhtv', qc * jnp.exp(G), S))
        G_end = G[:, :, -1, :]
        S = (jnp.exp(G_end)[..., None] * S
             + jnp.einsum('bhsd,bhsv->bhdv', kc * jnp.exp(G_end[:, :, None, :] - G), vc))
        return S, o

    S_fin, o = lax.scan(step, s0.astype(jnp.float32),
                        (chunks(q), chunks(k), chunks(v), chunks(log_f)))
    o = o.transpose(1, 0, 3, 2, 4).reshape(B, nc * c, H, DV)[:, :T]
    return o, S_fin


def hybrid_layer(x, past_k, past_v, conv_buf, hgrn_s, ffn_buf, past_len, p):
    B, T, _ = x.shape
    h = rmsnorm(x, p['norm_mix_g'])
    proj = h @ p['w_in']
    q_a, k_a, v_a, glu_in, f_c, i_c, q_c, g_c, gates = jnp.split(proj, split_offsets(), axis=-1)
    q_a = q_a.reshape(B, T, H_A, DH_A)
    k_a = k_a.reshape(B, T, H_A, DH_A)
    v_a = v_a.reshape(B, T, H_A, DH_A)
    q_pos = past_len + jnp.arange(T)
    o_a = stick_breaking_attention(q_a, q_pos, past_k + (k_a,), past_v + (v_a,), p['sb_bias'])
    y_a = o_a.reshape(B, T, D_A) @ p['w_out_a']
    u = glu_in[..., :D_B] * jax.nn.sigmoid(glu_in[..., D_B:])
    u_conv, new_conv = causal_depthwise_conv(u, conv_buf, p['conv_w'])
    y_b = jax.nn.silu(layernorm(u_conv + p['conv_b'], p['conv_ln_g'], p['conv_ln_b'])) @ p['w_out_b']
    lb = p['lb']
    zf = f_c.astype(jnp.float32)
    log_f = jnp.logaddexp(jnp.log(lb), jnp.log1p(-lb) + jax.nn.log_sigmoid(zf))
    k_c = (1.0 - lb) * jax.nn.sigmoid(-zf)
    o_c, new_s = hgrn2_chunked(q_c.reshape(B, T, H_C, DK_C), k_c.reshape(B, T, H_C, DK_C),
                               i_c.reshape(B, T, H_C, DV_C), log_f.reshape(B, T, H_C, DK_C), hgrn_s)
    o_c = rmsnorm(o_c, p['hgrn_norm_g'].reshape(H_C, DV_C)).reshape(B, T, D_CV).astype(x.dtype)
    y_c = (o_c * jax.nn.silu(g_c)) @ p['w_out_c']
    g = jax.nn.sigmoid(gates.reshape(B, T, N_BRANCH, D_MODEL))
    merged = g[:, :, 0] * y_a + g[:, :, 1] * y_b + g[:, :, 2] * y_c
    x = x + merged @ p['w_o']
    h2 = rmsnorm(x, p['norm_ffn_g'])
    a, new_ffn = causal_depthwise_conv(h2 @ p['w_gate'], ffn_buf, p['ffn_conv_w'])
    x = x + (jax.nn.silu(a) * (h2 @ p['w_up'])) @ p['w_down']
    return x, (k_a, v_a, new_conv, new_s.astype(hgrn_s.dtype), new_ffn)


def setup_inputs(seed: int = 0) -> dict:
    key = jax.random.key(seed)
    ks = jax.random.split(key, 32)
    n_pages = PAST_LEN // PAGE_SIZE
    n_used = DEC_BATCH * n_pages
    n_phys = n_used + (n_used + 3) // 4

    def nrm(k, shape, scale):
        return jax.random.normal(k, shape, jnp.float32) * scale

    def gain(k, shape):
        return 1.0 + nrm(k, shape, 0.02)

    page_table = jax.random.permutation(ks[7], n_phys)[:n_used].reshape(DEC_BATCH, n_pages).astype(jnp.int32)
    return {
        'x_prompt': nrm(ks[0], (BATCH, SEQ, D_MODEL), 1.0),
        'x_sample': nrm(ks[1], (DEC_BATCH, DEC_SEQ, D_MODEL), 1.0),
        'cache_k': nrm(ks[2], (DEPTH, n_phys, PAGE_SIZE, H_A, DH_A), 1.0),
        'cache_v': nrm(ks[3], (DEPTH, n_phys, PAGE_SIZE, H_A, DH_A), 1.0),
        'state_conv': nrm(ks[4], (DEPTH, DEC_BATCH, CONV_W - 1, D_B), 0.5),
        'state_hgrn': nrm(ks[5], (DEPTH, DEC_BATCH, H_C, DK_C, DV_C), 0.5),
        'state_ffn_conv': nrm(ks[6], (DEPTH, DEC_BATCH, FFN_CONV_W - 1, D_FF), 1.0),
        'page_table': page_table,
        'hgrn_lower_bound': nrm(ks[8], (DEPTH, D_C), 0.5),
        'sb_bias': SB_BIAS_INIT + nrm(ks[26], (DEPTH, H_A), 0.5),
        'norm_mix_g': gain(ks[9], (DEPTH, D_MODEL)),
        'w_in': nrm(ks[10], (DEPTH, D_MODEL, D_IN), D_MODEL ** -0.5),
        'conv_w': nrm(ks[11], (DEPTH, CONV_W, D_B), CONV_W ** -0.5),
        'conv_b': nrm(ks[12], (DEPTH, D_B), 0.02),
        'conv_ln_g': gain(ks[13], (DEPTH, D_B)),
        'conv_ln_b': nrm(ks[14], (DEPTH, D_B), 0.02),
        'hgrn_norm_g': gain(ks[15], (DEPTH, D_CV)),
        'w_out_a': nrm(ks[16], (DEPTH, D_A, D_MODEL), D_A ** -0.5),
        'w_out_b': nrm(ks[17], (DEPTH, D_B, D_MODEL), D_B ** -0.5),
        'w_out_c': nrm(ks[18], (DEPTH, D_CV, D_MODEL), D_CV ** -0.5),
        'w_o': nrm(ks[19], (DEPTH, D_MODEL, D_MODEL), D_MODEL ** -0.5),
        'norm_ffn_g': gain(ks[20], (DEPTH, D_MODEL)),
        'w_gate': nrm(ks[21], (DEPTH, D_MODEL, D_FF), D_MODEL ** -0.5),
        'w_up': nrm(ks[22], (DEPTH, D_MODEL, D_FF), D_MODEL ** -0.5),
        'ffn_conv_w': nrm(ks[23], (DEPTH, FFN_CONV_W, D_FF), FFN_CONV_W ** -0.5),
        'w_down': nrm(ks[24], (DEPTH, D_FF, D_MODEL), D_FF ** -0.5),
        'norm_final_g': gain(ks[25], (D_MODEL,)),
    }


def reference(x_prompt, x_sample, cache_k, cache_v, state_conv, state_hgrn, state_ffn_conv, page_table,
              hgrn_lower_bound, sb_bias, norm_mix_g, w_in, conv_w, conv_b, conv_ln_g, conv_ln_b, hgrn_norm_g,
              w_out_a, w_out_b, w_out_c, w_o, norm_ffn_g, w_gate, w_up, ffn_conv_w, w_down, norm_final_g):
    lbs = jnp.cumsum(jax.nn.softmax(hgrn_lower_bound.astype(jnp.float32), axis=0), axis=0)
    lbs = lbs - lbs[:1]
    past_len = page_table.shape[1] * cache_k.shape[2]
    Bp = x_prompt.shape[0]
    Bs = x_sample.shape[0]
    dt = x_prompt.dtype
    xp, xs = x_prompt, x_sample
    kp, vp, ksm, vsm, cp, csm, hp, hsm, fp, fsm = [], [], [], [], [], [], [], [], [], []
    for l in range(DEPTH):
        p = {'lb': lbs[l], 'sb_bias': sb_bias[l], 'norm_mix_g': norm_mix_g[l], 'w_in': w_in[l],
             'conv_w': conv_w[l], 'conv_b': conv_b[l], 'conv_ln_g': conv_ln_g[l], 'conv_ln_b': conv_ln_b[l],
             'hgrn_norm_g': hgrn_norm_g[l], 'w_out_a': w_out_a[l], 'w_out_b': w_out_b[l],
             'w_out_c': w_out_c[l], 'w_o': w_o[l], 'norm_ffn_g': norm_ffn_g[l], 'w_gate': w_gate[l],
             'w_up': w_up[l], 'ffn_conv_w': ffn_conv_w[l], 'w_down': w_down[l]}
        xp, (k1, v1, c1, h1, f1) = hybrid_layer(
            xp, (), (),
            jnp.zeros((Bp, CONV_W - 1, D_B), dt),
            jnp.zeros((Bp, H_C, DK_C, DV_C), dt),
            jnp.zeros((Bp, FFN_CONV_W - 1, D_FF), dt),
            0, p)
        past_k = cache_k[l][page_table].reshape(Bs, past_len, H_A, DH_A)
        past_v = cache_v[l][page_table].reshape(Bs, past_len, H_A, DH_A)
        xs, (k2, v2, c2, h2, f2) = hybrid_layer(
            xs, (past_k,), (past_v,), state_conv[l], state_hgrn[l], state_ffn_conv[l], past_len, p)
        kp.append(k1); vp.append(v1); cp.append(c1); hp.append(h1); fp.append(f1)
        ksm.append(k2); vsm.append(v2); csm.append(c2); hsm.append(h2); fsm.append(f2)
    y_prompt = rmsnorm(xp, norm_final_g)
    y_sample = rmsnorm(xs, norm_final_g)
    return (y_prompt, y_sample, jnp.stack(kp), jnp.stack(vp), jnp.stack(ksm), jnp.stack(vsm),
            jnp.stack(cp), jnp.stack(csm), jnp.stack(hp), jnp.stack(hsm), jnp.stack(fp), jnp.stack(fsm))
```

```python
import functools

import jax
import jax.numpy as jnp
from jax import lax
from jax.experimental import pallas as pl
from jax.experimental.pallas import tpu as pltpu

F32 = jnp.float32
BF16 = jnp.bfloat16
EPS = 1e-6

VMEM_LIMIT_BYTES = 56 * 1024 * 1024
SUBLANES = 8
LANES = 128

H_A = 16
DH_A = 64
H_C = 8
DK_C = 128
CONV_W = 31
FFN_CONV_W = 3
HGRN_CHUNK = 64
HGRN_SUB = 16
ATT_TQ = 128
ATT_TK = 256


def _cparams(*sem):
    return pltpu.CompilerParams(dimension_semantics=sem, vmem_limit_bytes=VMEM_LIMIT_BYTES)


def _pick(n, prefs):
    for p in prefs:
        if n % p == 0:
            return p
    return n


def _log_sigmoid_parts(z):
    l = jnp.log1p(jnp.exp(-jnp.abs(z)))
    return jnp.minimum(z, 0.0) - l, -(jnp.maximum(z, 0.0) + l)


def _silu(x):
    return x * jax.nn.sigmoid(x)


def _norm_mm_kernel(x_ref, g_ref, *refs, n_w):
    w_refs = refs[:n_w]
    o_refs = refs[n_w:2 * n_w]
    h_ref = refs[2 * n_w]

    @pl.when(pl.program_id(1) == 0)
    def _():
        xf = x_ref[...]
        r = lax.rsqrt(jnp.mean(xf * xf, axis=-1, keepdims=True) + EPS)
        h_ref[...] = ((xf * r) * g_ref[...]).astype(BF16)

    h = h_ref[...]
    for w_ref, o_ref in zip(w_refs, o_refs):
        o_ref[...] = jnp.dot(h, w_ref[...].astype(BF16), preferred_element_type=F32)


def norm_matmul(x, g, ws, layer, name):
    M, D = x.shape
    N = ws[0].shape[2]
    tm = _pick(M, (1024, 512, 256, 128))
    tn = _pick(N, (512, 256, 128))
    n_w = len(ws)
    g3 = g.reshape(g.shape[0], 1, D)
    outs = pl.pallas_call(
        functools.partial(_norm_mm_kernel, n_w=n_w),
        grid=(M // tm, N // tn),
        in_specs=[pl.BlockSpec((tm, D), lambda i, j: (i, 0)),
                  pl.BlockSpec((None, 1, D), lambda i, j: (layer, 0, 0))]
                 + [pl.BlockSpec((None, D, tn), lambda i, j: (layer, 0, j)) for _ in ws],
        out_specs=[pl.BlockSpec((tm, tn), lambda i, j: (i, j)) for _ in ws],
        out_shape=[jax.ShapeDtypeStruct((M, N), F32) for _ in ws],
        scratch_shapes=[pltpu.VMEM((tm, D), BF16)],
        compiler_params=_cparams("parallel", "arbitrary"),
        name=name,
    )(x, g3, *ws)
    return outs


def _rmsnorm_kernel(x_ref, g_ref, o_ref):
    xf = x_ref[...]
    r = lax.rsqrt(jnp.mean(xf * xf, axis=-1, keepdims=True) + EPS)
    o_ref[...] = (xf * r) * g_ref[...]


def rmsnorm_rows(x, g, name):
    M, D = x.shape
    tm = _pick(M, (512, 256, 128))
    return pl.pallas_call(
        _rmsnorm_kernel,
        grid=(M // tm,),
        in_specs=[pl.BlockSpec((tm, D), lambda i: (i, 0)),
                  pl.BlockSpec((1, D), lambda i: (0, 0))],
        out_specs=pl.BlockSpec((tm, D), lambda i: (i, 0)),
        out_shape=jax.ShapeDtypeStruct((M, D), F32),
        compiler_params=_cparams("parallel"),
        name=name,
    )(x, g.reshape(1, D))


def _mm_res_kernel(x_ref, a_ref, w_ref, o_ref):
    o_ref[...] = x_ref[...] + jnp.dot(a_ref[...], w_ref[...].astype(BF16),
                                      preferred_element_type=F32)


def matmul_residual(x, a, w, layer, name):
    M, N = x.shape
    K = a.shape[1]
    tm = _pick(M, (1024, 512, 256, 128))
    tn = 256 if K > 4096 else _pick(N, (512, 256, 128))
    return pl.pallas_call(
        _mm_res_kernel,
        grid=(M // tm, N // tn),
        in_specs=[pl.BlockSpec((tm, tn), lambda i, j: (i, j)),
                  pl.BlockSpec((tm, K), lambda i, j: (i, 0)),
                  pl.BlockSpec((None, K, tn), lambda i, j: (layer, 0, j))],
        out_specs=pl.BlockSpec((tm, tn), lambda i, j: (i, j)),
        out_shape=jax.ShapeDtypeStruct((M, N), F32),
        compiler_params=_cparams("parallel", "arbitrary"),
        name=name,
    )(x, a, w)


def _merge_kernel(oa_ref, ob_ref, oc_ref, g0_ref, g1_ref, g2_ref, wa_ref, wb_ref, wc_ref, o_ref):
    def branch(o_r, g_r, w_r):
        y = jnp.dot(o_r[...], w_r[...].astype(BF16), preferred_element_type=F32)
        return jax.nn.sigmoid(g_r[...]) * y

    m = branch(oa_ref, g0_ref, wa_ref) + branch(ob_ref, g1_ref, wb_ref) + branch(oc_ref, g2_ref, wc_ref)
    o_ref[...] = m.astype(BF16)


def merge_branches(oa, ob, oc, proj, gate_off, wa, wb, wc, layer, name):
    M, K = oa.shape
    N = wa.shape[2]
    tm = _pick(M, (1024, 512, 256, 128))
    tn = _pick(N, (512, 256, 128))
    gb = gate_off // tn
    nb = N // tn
    o_spec = pl.BlockSpec((tm, K), lambda i, j: (i, 0))
    w_spec = pl.BlockSpec((None, K, tn), lambda i, j: (layer, 0, j))
    return pl.pallas_call(
        _merge_kernel,
        grid=(M // tm, nb),
        in_specs=[o_spec, o_spec, o_spec,
                  pl.BlockSpec((tm, tn), lambda i, j: (i, gb + j)),
                  pl.BlockSpec((tm, tn), lambda i, j: (i, gb + nb + j)),
                  pl.BlockSpec((tm, tn), lambda i, j: (i, gb + 2 * nb + j)),
                  w_spec, w_spec, w_spec],
        out_specs=pl.BlockSpec((tm, tn), lambda i, j: (i, j)),
        out_shape=jax.ShapeDtypeStruct((M, N), BF16),
        compiler_params=_cparams("parallel", "arbitrary"),
        name=name,
    )(oa, ob, oc, proj, proj, proj, wa, wb, wc)


def _strict_upper(n):
    r = lax.broadcasted_iota(jnp.int32, (n, n), 0)
    c = lax.broadcasted_iota(jnp.int32, (n, n), 1)
    return jnp.where(r > c, 1.0, 0.0).astype(BF16)


def _sb_tile(z, v_bf, u_mat, cry, mask):
    log_beta, log_rem = _log_sigmoid_parts(z)
    if mask is not None:
        log_rem = jnp.where(mask, log_rem, 0.0)
    hi = log_rem.astype(BF16)
    lo = (log_rem - hi.astype(F32)).astype(BF16)
    rows = z.shape[0]
    cs = jnp.dot(jnp.concatenate([hi, lo], axis=0), u_mat, preferred_element_type=F32)
    suffix = cs[:rows] + cs[rows:]
    w = jnp.exp(log_beta + suffix + cry)
    if mask is not None:
        w = jnp.where(mask, w, 0.0)
    contrib = jnp.dot(w.astype(BF16), v_bf, preferred_element_type=F32)
    return contrib, cry + jnp.sum(log_rem, axis=-1, keepdims=True)


def _attn_prompt_kernel(bias_ref, q_ref, k_ref, v_ref, o_ref, kb_ref, vb_ref, *, layer, heads_per_blk):
    hp = pl.program_id(1)
    i = pl.program_id(2)
    tq = q_ref.shape[0]
    tk = ATT_TK
    dh = DH_A

    @pl.when(i == 0)
    def _():
        for hh in range(heads_per_blk):
            kb_ref[hh] = k_ref[:, hh * dh:(hh + 1) * dh].astype(BF16)
            vb_ref[hh] = v_ref[:, hh * dh:(hh + 1) * dh].astype(BF16)

    u_mat = _strict_upper(tk)
    j_last = (i * tq) // tk
    q_pos = i * tq + lax.broadcasted_iota(jnp.int32, (tq, tk), 0)
    k_pos = j_last * tk + lax.broadcasted_iota(jnp.int32, (tq, tk), 1)
    diag_mask = k_pos < q_pos

    for hh in range(heads_per_blk):
        bias = bias_ref[layer * H_A + hp * heads_per_blk + hh]
        q = (q_ref[:, hh * dh:(hh + 1) * dh] * (dh ** -0.5)).astype(BF16)

        def logits(j):
            start = pl.multiple_of(j * tk, tk)
            kt = kb_ref[hh, pl.ds(start, tk), :]
            vt = vb_ref[hh, pl.ds(start, tk), :]
            z = lax.dot_general(q, kt, (((1,), (1,)), ((), ())), preferred_element_type=F32) + bias
            return z, vt

        z, vt = logits(j_last)
        acc, cry = _sb_tile(z, vt, u_mat, jnp.zeros((tq, 1), F32), diag_mask)

        def body(jj, carry):
            acc, cry = carry
            z, vt = logits(j_last - 1 - jj)
            contrib, cry = _sb_tile(z, vt, u_mat, cry, None)
            return acc + contrib, cry

        acc, _ = lax.fori_loop(0, j_last, body, (acc, cry))
        o_ref[:, hh * dh:(hh + 1) * dh] = acc.astype(o_ref.dtype)


def attn_prompt(proj, sb_bias, layer, B, T, name):
    d_a = H_A * DH_A
    hpb = LANES // DH_A
    n_hp = H_A // hpb
    nq = T // ATT_TQ
    kv_blk = d_a // LANES
    return pl.pallas_call(
        functools.partial(_attn_prompt_kernel, layer=layer, heads_per_blk=hpb),
        grid=(B, n_hp, nq),
        in_specs=[pl.BlockSpec(memory_space=pltpu.SMEM),
                  pl.BlockSpec((ATT_TQ, LANES), lambda b, h, i: (b * nq + i, h)),
                  pl.BlockSpec((T, LANES), lambda b, h, i: (b, kv_blk + h)),
                  pl.BlockSpec((T, LANES), lambda b, h, i: (b, 2 * kv_blk + h))],
        out_specs=pl.BlockSpec((ATT_TQ, LANES), lambda b, h, i: (b * nq + i, h)),
        out_shape=jax.ShapeDtypeStruct((B * T, d_a), BF16),
        scratch_shapes=[pltpu.VMEM((hpb, T, DH_A), BF16), pltpu.VMEM((hpb, T, DH_A), BF16)],
        compiler_params=_cparams("parallel", "parallel", "arbitrary"),
        name=name,
    )(sb_bias.reshape(-1), proj, proj, proj)


def _attn_sample_kernel(pt_ref, q_ref, kn_ref, vn_ref, bias_ref, *refs, n_pg, page):
    del pt_ref
    k_refs = refs[:n_pg]
    v_refs = refs[n_pg:2 * n_pg]
    o_ref = refs[2 * n_pg]
    qbd_ref, acc_ref, cry_ref, kpad_ref, vpad_ref = refs[2 * n_pg + 1:]
    g = pl.program_id(1)
    ts, d_a = q_ref.shape
    rows = ts * H_A
    head_of_col = lax.broadcasted_iota(jnp.int32, (H_A, d_a), 1) // DH_A
    head_mask = head_of_col == lax.broadcasted_iota(jnp.int32, (H_A, d_a), 0)
    u_mat = _strict_upper(page)

    @pl.when(g == 0)
    def _():
        q = q_ref[...] * (DH_A ** -0.5)
        qbd = jnp.concatenate(
            [jnp.where(head_mask, jnp.broadcast_to(q[t:t + 1, :], (H_A, d_a)), 0.0) for t in range(ts)], axis=0)
        qbd_ref[...] = qbd.astype(BF16)
        kpad_ref[...] = jnp.zeros_like(kpad_ref)
        vpad_ref[...] = jnp.zeros_like(vpad_ref)
        kpad_ref[0:ts, :] = kn_ref[...]
        vpad_ref[0:ts, :] = vn_ref[...]
        kn = kpad_ref[...].astype(BF16)
        vn = vpad_ref[...].astype(BF16)
        z = lax.dot_general(qbd_ref[...], kn, (((1,), (1,)), ((), ())),
                            preferred_element_type=F32) + bias_ref[...]
        t_of_row = lax.broadcasted_iota(jnp.int32, (rows, page), 0) // H_A
        mask = lax.broadcasted_iota(jnp.int32, (rows, page), 1) < t_of_row
        contrib, cry = _sb_tile(z, vn, u_mat, jnp.zeros((rows, 1), F32), mask)
        acc_ref[...] = contrib
        cry_ref[...] = cry

    qbd = qbd_ref[...]
    for r in range(n_pg):
        z = lax.dot_general(qbd, k_refs[r][...].astype(BF16), (((1,), (1,)), ((), ())),
                            preferred_element_type=F32) + bias_ref[...]
        contrib, cry = _sb_tile(z, v_refs[r][...].astype(BF16), u_mat, cry_ref[...], None)
        acc_ref[...] += contrib
        cry_ref[...] = cry

    @pl.when(g == pl.num_programs(1) - 1)
    def _():
        hm = head_mask.astype(F32)
        for t in range(ts):
            o_ref[t:t + 1, :] = jnp.sum(acc_ref[t * H_A:(t + 1) * H_A, :] * hm, axis=0, keepdims=True)


def attn_sample(q, k_new, v_new, cache_k, cache_v, page_table, sb_bias, layer, name):
    Bs, Ts, d_a = q.shape
    n_pages = page_table.shape[1]
    page = cache_k.shape[2]
    n_pg = _pick(n_pages, (4, 2, 1))
    n_grp = n_pages // n_pg
    bias_col = jnp.tile(sb_bias[layer], Ts).reshape(Ts * H_A, 1)

    def page_spec(r):
        return pl.BlockSpec((None, None, page, d_a),
                            lambda b, g, pt: (layer, pt[b, n_pages - 1 - (g * n_pg + r)], 0, 0))

    row_spec = pl.BlockSpec((None, Ts, d_a), lambda b, g, pt: (b, 0, 0))
    grid_spec = pltpu.PrefetchScalarGridSpec(
        num_scalar_prefetch=1,
        grid=(Bs, n_grp),
        in_specs=[row_spec, row_spec, row_spec,
                  pl.BlockSpec((Ts * H_A, 1), lambda b, g, pt: (0, 0))]
                 + [page_spec(r) for r in range(n_pg)] + [page_spec(r) for r in range(n_pg)],
        out_specs=pl.BlockSpec((None, Ts, d_a), lambda b, g, pt: (b, 0, 0)),
        scratch_shapes=[pltpu.VMEM((Ts * H_A, d_a), BF16),
                        pltpu.VMEM((Ts * H_A, d_a), F32),
                        pltpu.VMEM((Ts * H_A, 1), F32),
                        pltpu.VMEM((page, d_a), F32),
                        pltpu.VMEM((page, d_a), F32)],
    )
    return pl.pallas_call(
        functools.partial(_attn_sample_kernel, n_pg=n_pg, page=page),
        grid_spec=grid_spec,
        out_shape=jax.ShapeDtypeStruct((Bs, Ts, d_a), F32),
        compiler_params=_cparams("parallel", "arbitrary"),
        name=name,
    )(page_table, q, k_new, v_new, bias_col, *([cache_k] * n_pg), *([cache_v] * n_pg))


def _conv_kernel(a_ref, b_ref, ha_ref, hb_ref, w_ref, cb_ref, lg_ref, lb_ref, y_ref, tail_ref,
                 ubuf, cbuf, *, stride, halo, tiles_per_seq, halo_is_state):
    tt = a_ref.shape[0]
    d_b = a_ref.shape[1]
    if halo_is_state:
        ubuf[0:halo, :] = ha_ref[...]
    else:
        first = (pl.program_id(0) % tiles_per_seq) == 0
        uh = ha_ref[...] * jax.nn.sigmoid(hb_ref[...])
        ubuf[0:halo, :] = jnp.where(first, 0.0, uh)
    ubuf[halo:halo + tt, :] = a_ref[...] * jax.nn.sigmoid(b_ref[...])

    base = halo - (CONV_W - 1) * stride
    for c in range(d_b // LANES):
        cols = slice(c * LANES, (c + 1) * LANES)
        acc = jnp.zeros((tt, LANES), F32)
        for j in range(CONV_W):
            s = base + j * stride
            acc = acc + w_ref[j:j + 1, cols] * ubuf[s:s + tt, cols]
        cbuf[:, cols] = acc

    cv = cbuf[...] + cb_ref[...]
    mu = jnp.mean(cv, axis=-1, keepdims=True)
    d = cv - mu
    var = jnp.mean(d * d, axis=-1, keepdims=True)
    y = (d * lax.rsqrt(var + EPS)) * lg_ref[...] + lb_ref[...]
    y_ref[...] = _silu(y).astype(y_ref.dtype)
    n_tail = tail_ref.shape[0]
    tail_ref[...] = ubuf[halo + tt - n_tail:halo + tt, :]


def conv_module(proj, glu_off, d_b, state, conv_w, conv_b, ln_g, ln_b, layer, seq_len, stride, name):
    M = proj.shape[0]
    ca = glu_off // d_b
    w = conv_w[layer]
    vecs = [conv_b[layer].reshape(1, d_b), ln_g[layer].reshape(1, d_b), ln_b[layer].reshape(1, d_b)]
    vec_spec = pl.BlockSpec((1, d_b), lambda i: (0, 0))
    if state is None:
        tt = _pick(seq_len, (256, 128))
        halo = 32
        tps = seq_len // tt
        n_seq = M // seq_len
        hblk = tt // halo
        halo_specs = [pl.BlockSpec((halo, d_b), lambda i: (jnp.maximum(i * hblk - 1, 0), ca)),
                      pl.BlockSpec((halo, d_b), lambda i: (jnp.maximum(i * hblk - 1, 0), ca + 1))]
        halo_args = [proj, proj]
        n_tail = halo
        tail_shape = (n_seq, n_tail, d_b)
        tail_spec = pl.BlockSpec((None, n_tail, d_b), lambda i: (i // tps, 0, 0))
        grid = (M // tt,)
    else:
        tt = M
        halo = state.shape[0]
        tps = 1
        halo_specs = [pl.BlockSpec((halo, d_b), lambda i: (0, 0)),
                      pl.BlockSpec((halo, d_b), lambda i: (0, 0))]
        halo_args = [state, state]
        n_tail = halo
        tail_shape = (n_tail, d_b)
        tail_spec = pl.BlockSpec((n_tail, d_b), lambda i: (0, 0))
        grid = (1,)
    return pl.pallas_call(
        functools.partial(_conv_kernel, stride=stride, halo=halo, tiles_per_seq=tps,
                          halo_is_state=state is not None),
        grid=grid,
        in_specs=[pl.BlockSpec((tt, d_b), lambda i: (i, ca)),
                  pl.BlockSpec((tt, d_b), lambda i: (i, ca + 1))] + halo_specs
                 + [pl.BlockSpec((CONV_W, d_b), lambda i: (0, 0)), vec_spec, vec_spec, vec_spec],
        out_specs=[pl.BlockSpec((tt, d_b), lambda i: (i, 0)), tail_spec],
        out_shape=[jax.ShapeDtypeStruct((M, d_b), BF16), jax.ShapeDtypeStruct(tail_shape, F32)],
        scratch_shapes=[pltpu.VMEM((halo + tt, d_b), F32), pltpu.VMEM((tt, d_b), F32)],
        compiler_params=_cparams("arbitrary"),
        name=name,
    )(proj, proj, *halo_args, w, *vecs)


def _hgrn_lower_bound(lbraw, layer):
    e = jnp.exp(lbraw - jnp.max(lbraw, axis=0, keepdims=True))
    p = e / jnp.sum(e, axis=0, keepdims=True)
    lb = jnp.zeros((1, lbraw.shape[1]), F32)
    for r in range(1, layer + 1):
        lb = lb + p[r:r + 1, :]
    return lb


def _cumsum_rows(x):
    n = x.shape[0]
    r = lax.broadcasted_iota(jnp.int32, (n, n), 0)
    c = lax.broadcasted_iota(jnp.int32, (n, n), 1)
    tri = jnp.where(r >= c, 1.0, 0.0).astype(BF16)
    hi = x.astype(BF16)
    r1 = x - hi.astype(F32)
    mid = r1.astype(BF16)
    lo = (r1 - mid.astype(F32)).astype(BF16)
    parts = jnp.dot(tri, jnp.concatenate([hi, mid, lo], axis=1), preferred_element_type=F32)
    w = x.shape[1]
    return parts[:, :w] + parts[:, w:2 * w] + parts[:, 2 * w:]


def _hgrn_chunk(zf, iv, q, gate, lb, gn, st, *, sub, n_valid):
    c = zf.shape[0]
    log_sig, log_sig_neg = _log_sigmoid_parts(zf)
    a_ = jnp.log(lb)
    b_ = jnp.log1p(-lb) + log_sig
    log_f = jnp.maximum(a_, b_) + jnp.log1p(jnp.exp(-jnp.abs(a_ - b_)))
    k = (1.0 - lb) * jnp.exp(log_sig_neg)
    if n_valid < c:
        valid = lax.broadcasted_iota(jnp.int32, zf.shape, 0) < n_valid
        log_f = jnp.where(valid, log_f, 0.0)
        k = jnp.where(valid, k, 0.0)
    G = _cumsum_rows(log_f)
    st_bf = st.astype(BF16)
    v_bf = iv.astype(BF16)
    row_c = lax.broadcasted_iota(jnp.int32, (c, 1), 0)
    outs = []
    for I in range(c // sub):
        lo, hi = I * sub, (I + 1) * sub
        if lo >= n_valid:
            outs.append(jnp.zeros((sub, iv.shape[1]), F32))
            continue
        qI, GI = q[lo:hi], G[lo:hi]
        o = lax.dot_general((qI * jnp.exp(GI)).astype(BF16), st_bf, (((1,), (1,)), ((), ())),
                            preferred_element_type=F32)
        row_l = lax.broadcasted_iota(jnp.int32, (sub, 1), 0)
        for s in range(min(sub, n_valid - lo)):
            sg = lo + s
            e = jnp.exp(jnp.minimum(GI - G[sg:sg + 1], 0.0))
            col = jnp.sum(qI * (k[sg:sg + 1] * e), axis=-1, keepdims=True)
            o = o + jnp.where(row_l >= s, col, 0.0) * iv[sg:sg + 1]
        if I > 0:
            ref = G[lo - 1:lo]
            qd = qI * jnp.exp(GI - ref)
            kd = jnp.where(row_c < lo, k * jnp.exp(jnp.minimum(ref - G, 0.0)), 0.0)
            A = lax.dot_general(qd.astype(BF16), kd.astype(BF16), (((1,), (1,)), ((), ())),
                                preferred_element_type=F32)
            o = o + jnp.dot(A.astype(BF16), v_bf, preferred_element_type=F32)
        outs.append(o)
    o = jnp.concatenate(outs, axis=0) if len(outs) > 1 else outs[0]
    g_end = G[c - 1:c]
    kdec = k * jnp.exp(g_end - G)
    st_new = st * jnp.exp(g_end) + lax.dot_general(v_bf, kdec.astype(BF16), (((0,), (0,)), ((), ())),
                                                  preferred_element_type=F32)
    r = lax.rsqrt(jnp.mean(o * o, axis=-1, keepdims=True) + EPS)
    y = ((o * r) * gn) * _silu(gate)
    return y, st_new


def _hgrn_prompt_kernel(f_ref, i_ref, q_ref, g_ref, lb_ref, gn_ref, y_ref, s_ref, st_ref, *, layer):
    T = f_ref.shape[0]
    c = HGRN_CHUNK
    lb = _hgrn_lower_bound(lb_ref[...], layer)
    gn = gn_ref[...]
    st_ref[...] = jnp.zeros_like(st_ref)

    def body(ci, carry):
        rows = pl.ds(pl.multiple_of(ci * c, c), c)
        y, st_new = _hgrn_chunk(f_ref[rows, :], i_ref[rows, :], q_ref[rows, :], g_ref[rows, :],
                                lb, gn, st_ref[...], sub=HGRN_SUB, n_valid=c)
        y_ref[rows, :] = y.astype(y_ref.dtype)
        st_ref[...] = st_new
        return carry

    lax.fori_loop(0, T // c, body, 0)
    s_ref[...] = st_ref[...].T


def hgrn_prompt(proj, f_off, lb_raw, norm_g, layer, B, T, name):
    d_c = H_C * DK_C
    fb = f_off // DK_C
    nh = H_C

    def col(k):
        return pl.BlockSpec((T, DK_C), lambda b, h: (b, fb + k * nh + h))

    return pl.pallas_call(
        functools.partial(_hgrn_prompt_kernel, layer=layer),
        grid=(B, H_C),
        in_specs=[col(0), col(1), col(2), col(3),
                  pl.BlockSpec((lb_raw.shape[0], DK_C), lambda b, h: (0, h)),
                  pl.BlockSpec((1, DK_C), lambda b, h: (0, h))],
        out_specs=[pl.BlockSpec((T, DK_C), lambda b, h: (b, h)),
                   pl.BlockSpec((None, None, DK_C, DK_C), lambda b, h: (b, h, 0, 0))],
        out_shape=[jax.ShapeDtypeStruct((B * T, d_c), BF16),
                   jax.ShapeDtypeStruct((B, H_C, DK_C, DK_C), F32)],
        scratch_shapes=[pltpu.VMEM((DK_C, DK_C), F32)],
        compiler_params=_cparams("parallel", "parallel"),
        name=name,
    )(proj, proj, proj, proj, lb_raw, norm_g[layer].reshape(1, d_c))


def _hgrn_sample_kernel(f_ref, i_ref, q_ref, g_ref, lb_ref, gn_ref, s0_ref, y_ref, s_ref, *, layer, ts):
    lb = _hgrn_lower_bound(lb_ref[...], layer)
    y, st_new = _hgrn_chunk(f_ref[...], i_ref[...], q_ref[...], g_ref[...], lb, gn_ref[...],
                            s0_ref[...].T, sub=f_ref.shape[0], n_valid=ts)
    y_ref[...] = y.astype(y_ref.dtype)
    s_ref[...] = st_new.T


def hgrn_sample(fiqg, state, lb_raw, norm_g, layer, ts, name):
    _, Bs, tp, d_c = fiqg.shape

    def col(k):
        return pl.BlockSpec((None, None, tp, DK_C), lambda b, h: (k, b, 0, h))

    st_spec = pl.BlockSpec((None, None, DK_C, DK_C), lambda b, h: (b, h, 0, 0))
    return pl.pallas_call(
        functools.partial(_hgrn_sample_kernel, layer=layer, ts=ts),
        grid=(Bs, H_C),
        in_specs=[col(0), col(1), col(2), col(3),
                  pl.BlockSpec((lb_raw.shape[0], DK_C), lambda b, h: (0, h)),
                  pl.BlockSpec((1, DK_C), lambda b, h: (0, h)),
                  st_spec],
        out_specs=[pl.BlockSpec((None, tp, DK_C), lambda b, h: (b, 0, h)), st_spec],
        out_shape=[jax.ShapeDtypeStruct((Bs, tp, d_c), BF16),
                   jax.ShapeDtypeStruct(state.shape, F32)],
        compiler_params=_cparams("parallel", "parallel"),
        name=name,
    )(fiqg, fiqg, fiqg, fiqg, lb_raw, norm_g[layer].reshape(1, d_c), state)


def _ffn_act_kernel(g_ref, u_ref, h_ref, w_ref, o_ref, gbuf, *, stride, halo, tiles_per_seq, halo_is_state):
    tt = g_ref.shape[0]
    if halo_is_state:
        gbuf[0:halo, :] = h_ref[...]
    else:
        first = (pl.program_id(1) % tiles_per_seq) == 0
        gbuf[0:halo, :] = jnp.where(first, 0.0, h_ref[...])
    gbuf[halo:halo + tt, :] = g_ref[...]
    a = w_ref[FFN_CONV_W - 1:FFN_CONV_W, :] * g_ref[...]
    for j in range(FFN_CONV_W - 1):
        s = halo - (FFN_CONV_W - 1 - j) * stride
        a = a + w_ref[j:j + 1, :] * gbuf[s:s + tt, :]
    o_ref[...] = (_silu(a) * u_ref[...]).astype(o_ref.dtype)


def ffn_act(G, U, state, conv_w, layer, seq_len, stride, name):
    M, d_ff = G.shape
    tc = _pick(d_ff, (1408, 512, 256, 128))
    w = conv_w[layer]
    if state is None:
        tt = _pick(seq_len, (512, 256, 128))
        halo = SUBLANES
        tps = seq_len // tt
        hblk = tt // halo
        halo_spec = pl.BlockSpec((halo, tc), lambda c, i: (jnp.maximum(i * hblk - 1, 0), c))
        halo_arg = G
    else:
        tt = M
        halo = state.shape[0]
        tps = 1
        halo_spec = pl.BlockSpec((halo, tc), lambda c, i: (0, c))
        halo_arg = state
    return pl.pallas_call(
        functools.partial(_ffn_act_kernel, stride=stride, halo=halo, tiles_per_seq=tps,
                          halo_is_state=state is not None),
        grid=(d_ff // tc, M // tt),
        in_specs=[pl.BlockSpec((tt, tc), lambda c, i: (i, c)),
                  pl.BlockSpec((tt, tc), lambda c, i: (i, c)),
                  halo_spec,
                  pl.BlockSpec((FFN_CONV_W, tc), lambda c, i: (0, c))],
        out_specs=pl.BlockSpec((tt, tc), lambda c, i: (i, c)),
        out_shape=jax.ShapeDtypeStruct((M, d_ff), BF16),
        scratch_shapes=[pltpu.VMEM((halo + tt, tc), F32)],
        compiler_params=_cparams("parallel", "parallel"),
        name=name,
    )(G, U, halo_arg, w)


def kernel(x_prompt, x_sample, cache_k, cache_v, state_conv, state_hgrn, state_ffn_conv, page_table, hgrn_lower_bound, sb_bias, norm_mix_g, w_in, conv_w, conv_b, conv_ln_g, conv_ln_b, hgrn_norm_g, w_out_a, w_out_b, w_out_c, w_o, norm_ffn_g, w_gate, w_up, ffn_conv_w, w_down, norm_final_g):
    Bp, T, D = x_prompt.shape
    Bs, Ts, _ = x_sample.shape
    depth = w_in.shape[0]
    d_a = H_A * DH_A
    d_b = conv_w.shape[2]
    d_c = H_C * DK_C
    d_ff = w_gate.shape[2]
    n_phys, page = cache_k.shape[1], cache_k.shape[2]
    off_glu = 3 * d_a
    off_f = off_glu + 2 * d_b
    off_gates = off_f + 4 * d_c
    tp = 16

    ck = cache_k.reshape(depth, n_phys, page, d_a)
    cv = cache_v.reshape(depth, n_phys, page, d_a)
    xp = x_prompt.reshape(Bp * T, D)
    xs = x_sample.transpose(1, 0, 2).reshape(Ts * Bs, D)

    def to_batch_major(a):
        return a.reshape(Ts, Bs, a.shape[-1]).transpose(1, 0, 2)

    def to_time_major(a):
        return a.transpose(1, 0, 2).reshape(Ts * Bs, a.shape[-1])

    outs = {k: [] for k in ("kp", "vp", "ks", "vs", "cp", "cs", "hp", "hs", "fp", "fs")}
    for l in range(depth):
        tag = f"l{l}"
        (pp,) = norm_matmul(xp, norm_mix_g, [w_in], l, f"in_p_{tag}")
        oa = attn_prompt(pp, sb_bias, l, Bp, T, f"attn_p_{tag}")
        ob, tail = conv_module(pp, off_glu, d_b, None, conv_w, conv_b, conv_ln_g, conv_ln_b, l, T, 1,
                               f"conv_p_{tag}")
        oc, hst = hgrn_prompt(pp, off_f, hgrn_lower_bound, hgrn_norm_g, l, Bp, T, f"hgrn_p_{tag}")
        mg = merge_branches(oa, ob, oc, pp, off_gates, w_out_a, w_out_b, w_out_c, l, f"merge_p_{tag}")
        xp = matmul_residual(xp, mg, w_o, l, f"wo_p_{tag}")
        gp, up = norm_matmul(xp, norm_ffn_g, [w_gate, w_up], l, f"ffn_in_p_{tag}")
        act = ffn_act(gp, up, None, ffn_conv_w, l, T, 1, f"ffn_act_p_{tag}")
        xp = matmul_residual(xp, act, w_down, l, f"down_p_{tag}")
        outs["kp"].append(pp[:, d_a:2 * d_a].reshape(Bp, T, H_A, DH_A))
        outs["vp"].append(pp[:, 2 * d_a:3 * d_a].reshape(Bp, T, H_A, DH_A))
        outs["cp"].append(tail[:, tail.shape[1] - (CONV_W - 1):, :])
        outs["hp"].append(hst)
        outs["fp"].append(gp.reshape(Bp, T, d_ff)[:, T - (FFN_CONV_W - 1):, :])

        (ps,) = norm_matmul(xs, norm_mix_g, [w_in], l, f"in_s_{tag}")
        qkv = to_batch_major(ps[:, :3 * d_a])
        q_s, k_s, v_s = qkv[..., :d_a], qkv[..., d_a:2 * d_a], qkv[..., 2 * d_a:]
        oa = attn_sample(q_s, k_s, v_s, ck, cv, page_table, sb_bias, l, f"attn_s_{tag}")
        oa = to_time_major(oa).astype(BF16)
        cst =state_conv[l].transpose(1, 0, 2).reshape((CONV_W - 1) * Bs, d_b)
        ob, tail = conv_module(ps, off_glu, d_b, cst, conv_w, conv_b, conv_ln_g, conv_ln_b, l, Ts, Bs,
                               f"conv_s_{tag}")
        fiqg = to_batch_major(ps[:, off_f:off_f + 4 * d_c]).reshape(Bs, Ts, 4, d_c)
        fiqg = jnp.pad(fiqg.transpose(2, 0, 1, 3), ((0, 0), (0, 0), (0, tp - Ts), (0, 0)))
        oc, hst = hgrn_sample(fiqg, state_hgrn[l], hgrn_lower_bound, hgrn_norm_g, l, Ts, f"hgrn_s_{tag}")
        oc = to_time_major(oc[:, :Ts, :])
        mg = merge_branches(oa, ob, oc, ps, off_gates, w_out_a, w_out_b, w_out_c, l, f"merge_s_{tag}")
        xs = matmul_residual(xs, mg, w_o, l, f"wo_s_{tag}")
        gs, us = norm_matmul(xs, norm_ffn_g, [w_gate, w_up], l, f"ffn_in_s_{tag}")
        fst = state_ffn_conv[l].transpose(1, 0, 2).reshape((FFN_CONV_W - 1) * Bs, d_ff)
        act = ffn_act(gs, us, fst, ffn_conv_w, l, Ts, Bs, f"ffn_act_s_{tag}")
        xs = matmul_residual(xs, act, w_down, l, f"down_s_{tag}")
        outs["ks"].append(k_s.reshape(Bs, Ts, H_A, DH_A))
        outs["vs"].append(v_s.reshape(Bs, Ts, H_A, DH_A))
        outs["cs"].append(tail.reshape(CONV_W - 1, Bs, d_b).transpose(1, 0, 2))
        outs["hs"].append(hst)
        ext_g = jnp.concatenate([fst, gs], axis=0)
        outs["fs"].append(ext_g[ext_g.shape[0] - (FFN_CONV_W - 1) * Bs:]
                          .reshape(FFN_CONV_W - 1, Bs, d_ff).transpose(1, 0, 2))

    y_p = rmsnorm_rows(xp, norm_final_g, "final_p").reshape(Bp, T, D)
    y_s = to_batch_major(rmsnorm_rows(xs, norm_final_g, "final_s"))
    st = lambda k: jnp.stack(outs[k])
    return (y_p, y_s, st("kp"), st("vp"), st("ks"), st("vs"), st("cp"), st("cs"),
            st("hp"), st("hs"), st("fp"), st("fs"))
```

```python
import functools

import jax
import jax.numpy as jnp
from jax import lax
from jax.experimental import pallas as pl
from jax.experimental.pallas import tpu as pltpu

F32 = jnp.float32
BF16 = jnp.bfloat16
EPS = 1e-6

VMEM_LIMIT_BYTES = 56 * 1024 * 1024
SUBLANES = 8
LANES = 128

H_A = 16
DH_A = 64
H_C = 8
DK_C = 128
CONV_W = 31
FFN_CONV_W = 3
HGRN_CHUNK = 64
HGRN_SUB = 16
ATT_TQ = 256
ATT_TK = 256
ATT_HEADS_PER_STEP = 4


def _cparams(*sem):
    return pltpu.CompilerParams(dimension_semantics=sem, vmem_limit_bytes=VMEM_LIMIT_BYTES)


def _pick(n, prefs):
    for p in prefs:
        if n % p == 0:
            return p
    return n


def _log_sigmoid_parts(z):
    l = jnp.log1p(jnp.exp(-jnp.abs(z)))
    return jnp.minimum(z, 0.0) - l, -(jnp.maximum(z, 0.0) + l)


def _silu(x):
    return x * jax.nn.sigmoid(x)


def _norm_mm_kernel(x_ref, g_ref, *refs, n_w):
    w_refs = refs[:n_w]
    o_refs = refs[n_w:2 * n_w]
    h_ref = refs[2 * n_w]

    @pl.when(pl.program_id(1) == 0)
    def _():
        xf = x_ref[...]
        r = lax.rsqrt(jnp.mean(xf * xf, axis=-1, keepdims=True) + EPS)
        h_ref[...] = ((xf * r) * g_ref[...]).astype(BF16)

    h = h_ref[...]
    for w_ref, o_ref in zip(w_refs, o_refs):
        o_ref[...] = jnp.dot(h, w_ref[...].astype(BF16), preferred_element_type=F32)


def norm_matmul(x, g, ws, layer, name):
    M, D = x.shape
    N = ws[0].shape[2]
    tm = _pick(M, (1024, 512, 256, 128))
    tn = _pick(N, (512, 256, 128))
    n_w = len(ws)
    g3 = g.reshape(g.shape[0], 1, D)
    outs = pl.pallas_call(
        functools.partial(_norm_mm_kernel, n_w=n_w),
        grid=(M // tm, N // tn),
        in_specs=[pl.BlockSpec((tm, D), lambda i, j: (i, 0)),
                  pl.BlockSpec((None, 1, D), lambda i, j: (layer, 0, 0))]
                 + [pl.BlockSpec((None, D, tn), lambda i, j: (layer, 0, j)) for _ in ws],
        out_specs=[pl.BlockSpec((tm, tn), lambda i, j: (i, j)) for _ in ws],
        out_shape=[jax.ShapeDtypeStruct((M, N), F32) for _ in ws],
        scratch_shapes=[pltpu.VMEM((tm, D), BF16)],
        compiler_params=_cparams("parallel", "arbitrary"),
        name=name,
    )(x, g3, *ws)
    return outs


def _rmsnorm_kernel(x_ref, g_ref, o_ref):
    xf = x_ref[...]
    r = lax.rsqrt(jnp.mean(xf * xf, axis=-1, keepdims=True) + EPS)
    o_ref[...] = (xf * r) * g_ref[...]


def rmsnorm_rows(x, g, name):
    M, D = x.shape
    tm = _pick(M, (512, 256, 128))
    return pl.pallas_call(
        _rmsnorm_kernel,
        grid=(M // tm,),
        in_specs=[pl.BlockSpec((tm, D), lambda i: (i, 0)),
                  pl.BlockSpec((1, D), lambda i: (0, 0))],
        out_specs=pl.BlockSpec((tm, D), lambda i: (i, 0)),
        out_shape=jax.ShapeDtypeStruct((M, D), F32),
        compiler_params=_cparams("parallel"),
        name=name,
    )(x, g.reshape(1, D))


def _mm_res_kernel(x_ref, a_ref, w_ref, o_ref):
    o_ref[...] = x_ref[...] + jnp.dot(a_ref[...], w_ref[...].astype(BF16),
                                      preferred_element_type=F32)


def matmul_residual(x, a, w, layer, name):
    M, N = x.shape
    K = a.shape[1]
    tm = _pick(M, (1024, 512, 256, 128))
    tn = 256 if K > 4096 else _pick(N, (512, 256, 128))
    return pl.pallas_call(
        _mm_res_kernel,
        grid=(M // tm, N // tn),
        in_specs=[pl.BlockSpec((tm, tn), lambda i, j: (i, j)),
                  pl.BlockSpec((tm, K), lambda i, j: (i, 0)),
                  pl.BlockSpec((None, K, tn), lambda i, j: (layer, 0, j))],
        out_specs=pl.BlockSpec((tm, tn), lambda i, j: (i, j)),
        out_shape=jax.ShapeDtypeStruct((M, N), F32),
        compiler_params=_cparams("parallel", "arbitrary"),
        name=name,
    )(x, a, w)


def _merge_kernel(oa_ref, ob_ref, oc_ref, g0_ref, g1_ref, g2_ref, wa_ref, wb_ref, wc_ref, o_ref):
    def branch(o_r, g_r, w_r):
        y = jnp.dot(o_r[...], w_r[...].astype(BF16), preferred_element_type=F32)
        return jax.nn.sigmoid(g_r[...]) * y

    m = branch(oa_ref, g0_ref, wa_ref) + branch(ob_ref, g1_ref, wb_ref) + branch(oc_ref, g2_ref, wc_ref)
    o_ref[...] = m.astype(BF16)


def merge_branches(oa, ob, oc, proj, gate_off, wa, wb, wc, layer, name):
    M, K = oa.shape
    N = wa.shape[2]
    tm = _pick(M, (1024, 512, 256, 128))
    tn = _pick(N, (512, 256, 128))
    gb = gate_off // tn
    nb = N // tn
    o_spec = pl.BlockSpec((tm, K), lambda i, j: (i, 0))
    w_spec = pl.BlockSpec((None, K, tn), lambda i, j: (layer, 0, j))
    return pl.pallas_call(
        _merge_kernel,
        grid=(M // tm, nb),
        in_specs=[o_spec, o_spec, o_spec,
                  pl.BlockSpec((tm, tn), lambda i, j: (i, gb + j)),
                  pl.BlockSpec((tm, tn), lambda i, j: (i, gb + nb + j)),
                  pl.BlockSpec((tm, tn), lambda i, j: (i, gb + 2 * nb + j)),
                  w_spec, w_spec, w_spec],
        out_specs=pl.BlockSpec((tm, tn), lambda i, j: (i, j)),
        out_shape=jax.ShapeDtypeStruct((M, N), BF16),
        compiler_params=_cparams("parallel", "arbitrary"),
        name=name,
    )(oa, ob, oc, proj, proj, proj, wa, wb, wc)


LOG2E = 1.4426950408889634


def _strict_upper2(n):
    r = lax.broadcasted_iota(jnp.int32, (2 * n, n), 0)
    c = lax.broadcasted_iota(jnp.int32, (2 * n, n), 1)
    r = jnp.where(r >= n, r - n, r)
    return jnp.where(r > c, 1.0, 0.0).astype(BF16)


def _neg_abs(x):
    bits = pltpu.bitcast(x, jnp.uint32) | jnp.uint32(0x80000000)
    return pltpu.bitcast(bits, F32)


MASKED_LOGIT = -1e30


def _sb_tiles(z2s, v_bfs, u2, crys, chained=False):
    staged = []
    for z2 in z2s:
        l2 = jnp.log2(1.0 + jnp.exp2(_neg_abs(z2)))
        log_beta = jnp.minimum(z2, 0.0) - l2
        neg_log_rem = jnp.maximum(z2, 0.0) + l2
        hi = neg_log_rem.astype(BF16)
        lo = (neg_log_rem - hi.astype(F32)).astype(BF16)
        suffix = jnp.dot(jnp.concatenate([hi, lo], axis=1), u2, preferred_element_type=F32)
        staged.append((log_beta, suffix, jnp.sum(neg_log_rem, axis=-1, keepdims=True)))
    contribs, new_crys = [], []
    for n, ((log_beta, suffix, row_sum), v_bf) in enumerate(zip(staged, v_bfs)):
        cry = crys if chained else crys[n]
        w = jnp.exp2(log_beta - suffix - cry)
        contribs.append(jnp.dot(w.astype(BF16), v_bf, preferred_element_type=F32))
        if chained:
            crys = cry + row_sum
        else:
            new_crys.append(cry + row_sum)
    return contribs, (crys if chained else new_crys)


def _attn_prompt_kernel(bias_ref, q_ref, k_ref, v_ref, o_ref, kb_ref, vb_ref, acc_ref, cry_ref, *,
                        layer, heads_per_blk):
    hp = pl.program_id(1)
    i = pl.program_id(2)
    tq = q_ref.shape[0]
    tk = ATT_TK
    dh = DH_A

    heads = range(heads_per_blk)

    @pl.when(i == 0)
    def _():
        lane = lax.broadcasted_iota(jnp.int32, (k_ref.shape[0], dh), 1)
        for hh in heads:
            b2 = jnp.full(lane.shape, bias_ref[layer * H_A + hp * heads_per_blk + hh] * LOG2E, F32)
            b_hi = b2.astype(BF16).astype(F32)
            extra = jnp.where(lane == 0, b_hi, jnp.where(lane == 1, b2 - b_hi, 0.0)).astype(BF16)
            kb_ref[hh] = jnp.concatenate([k_ref[:, hh * dh:(hh + 1) * dh].astype(BF16), extra], axis=1)
            vb_ref[hh] = v_ref[:, hh * dh:(hh + 1) * dh].astype(BF16)

    u2 = _strict_upper2(tk)
    diag_mask = (lax.broadcasted_iota(jnp.int32, (tq, tk), 1)
                 < lax.broadcasted_iota(jnp.int32, (tq, tk), 0))
    ones2 = jnp.where(lax.broadcasted_iota(jnp.int32, (tq, dh), 1) < 2, 1.0, 0.0).astype(BF16)
    q2 = [jnp.concatenate([(q_ref[:, hh * dh:(hh + 1) * dh] * (dh ** -0.5 * LOG2E)).astype(BF16), ones2],
                          axis=1) for hh in heads]

    def tile(j, mask, first):
        rows = pl.ds(pl.multiple_of(j * tk, tk), tk)
        z2s = [lax.dot_general(q2[hh], kb_ref[hh, rows, :], (((1,), (1,)), ((), ())),
                               preferred_element_type=F32) for hh in heads]
        if mask is not None:
            z2s = [jnp.where(mask, z2, MASKED_LOGIT) for z2 in z2s]
        crys = [jnp.zeros((tq, 1), F32) if first else cry_ref[hh] for hh in heads]
        contribs, crys = _sb_tiles(z2s, [vb_ref[hh, rows, :] for hh in heads], u2, crys)
        for hh in heads:
            acc_ref[hh] = contribs[hh] if first else acc_ref[hh] + contribs[hh]
            cry_ref[hh] = crys[hh]

    tile(i, diag_mask, True)

    def body(jj, carry):
        tile(i - 1 - jj, None, False)
        return carry

    lax.fori_loop(0, i, body, 0)
    for hh in heads:
        o_ref[:, hh * dh:(hh + 1) * dh] = acc_ref[hh].astype(o_ref.dtype)


def attn_prompt(proj, sb_bias, layer, B, T, name):
    d_a = H_A * DH_A
    hpb = ATT_HEADS_PER_STEP
    wb = hpb * DH_A
    n_hp = H_A // hpb
    nq = T // ATT_TQ
    kv_blk = d_a // wb
    return pl.pallas_call(
        functools.partial(_attn_prompt_kernel, layer=layer, heads_per_blk=hpb),
        grid=(B, n_hp, nq),
        in_specs=[pl.BlockSpec(memory_space=pltpu.SMEM),
                  pl.BlockSpec((ATT_TQ, wb), lambda b, h, i: (b * nq + i, h)),
                  pl.BlockSpec((T, wb), lambda b, h, i: (b, kv_blk + h)),
                  pl.BlockSpec((T, wb), lambda b, h, i: (b, 2 * kv_blk + h))],
        out_specs=pl.BlockSpec((ATT_TQ, wb), lambda b, h, i: (b * nq + i, h)),
        out_shape=jax.ShapeDtypeStruct((B * T, d_a), BF16),
        scratch_shapes=[pltpu.VMEM((hpb, T, 2 * DH_A), BF16), pltpu.VMEM((hpb, T, DH_A), BF16),
                        pltpu.VMEM((hpb, ATT_TQ, DH_A), F32), pltpu.VMEM((hpb, ATT_TQ, 1), F32)],
        compiler_params=_cparams("parallel", "parallel", "arbitrary"),
        name=name,
    )(sb_bias.reshape(-1), proj, proj, proj)


ATT_S_BLK = 256


def _attn_sample_kernel(pt_ref, q_ref, kn_ref, vn_ref, bias_ref, *refs, n_pg):
    del pt_ref
    k_refs = refs[:n_pg]
    v_refs = refs[n_pg:2 * n_pg]
    o_ref = refs[2 * n_pg]
    acc_ref, cry_ref, bm_ref = refs[2 * n_pg + 1:]
    g = pl.program_id(1)
    rows, dh = q_ref.shape
    n_exp = k_refs[0].shape[0] * k_refs[0].shape[1]
    blk = ATT_S_BLK
    q2 = (q_ref[...] * (dh ** -0.5 * LOG2E)).astype(BF16)
    bias2 = bias_ref[...] * LOG2E
    nt = (((1,), (1,)), ((), ()))

    def head_mask(n):
        r = lax.broadcasted_iota(jnp.int32, (rows, n), 0)
        c = lax.broadcasted_iota(jnp.int32, (rows, n), 1)
        return r, c, (r % H_A) == (c % H_A)

    @pl.when(g == 0)
    def _():
        n_new = kn_ref.shape[0]
        r, c, same_head = head_mask(n_new)
        mask = jnp.logical_and(same_head, (c // H_A) < (r // H_A))
        z2 = lax.dot_general(q2, kn_ref[...].astype(BF16), nt, preferred_element_type=F32) + bias2
        contribs, crys = _sb_tiles([jnp.where(mask, z2, MASKED_LOGIT)], [vn_ref[...].astype(BF16)],
                                   _strict_upper2(n_new), [jnp.zeros((rows, 1), F32)])
        acc_ref[...] = contribs[0]
        cry_ref[...] = crys[0]
        _, _, same_head = head_mask(n_exp)
        bm_ref[...] = jnp.where(same_head, bias2, MASKED_LOGIT)

    u2 = _strict_upper2(blk)
    z2s, vts = [], []
    for r in range(n_pg):
        kf = k_refs[r][...].reshape(n_exp, dh).astype(BF16)
        vf = v_refs[r][...].reshape(n_exp, dh).astype(BF16)
        z2_all = lax.dot_general(q2, kf, nt, preferred_element_type=F32) + bm_ref[...]
        for cb in reversed(range(n_exp // blk)):
            cols = slice(cb * blk, (cb + 1) * blk)
            z2s.append(z2_all[:, cols])
            vts.append(vf[cols, :])
    contribs, cry = _sb_tiles(z2s, vts, u2, cry_ref[...], chained=True)
    acc = acc_ref[...]
    for contrib in contribs:
        acc = acc + contrib
    acc_ref[...] = acc
    cry_ref[...] = cry

    @pl.when(g == pl.num_programs(1) - 1)
    def _():
        o_ref[...] = acc


def attn_sample(q, k_new, v_new, cache_k, cache_v, page_table, sb_bias, layer, name):
    Bs, rows, dh = q.shape
    n_pages = page_table.shape[1]
    page, n_h = cache_k.shape[2], cache_k.shape[3]
    n_pg = _pick(n_pages, (4, 2, 1))
    n_grp = n_pages // n_pg
    bias_col = jnp.tile(sb_bias[layer], rows // n_h).reshape(rows, 1)

    def page_spec(r):
        return pl.BlockSpec((None, None, page, n_h, dh),
                            lambda b, g, pt: (layer, pt[b, n_pages - 1 - (g * n_pg + r)], 0, 0, 0))

    row_spec = pl.BlockSpec((None, rows, dh), lambda b, g, pt: (b, 0, 0))
    grid_spec = pltpu.PrefetchScalarGridSpec(
        num_scalar_prefetch=1,
        grid=(Bs, n_grp),
        in_specs=[row_spec, row_spec, row_spec,
                  pl.BlockSpec((rows, 1), lambda b, g, pt: (0, 0))]
                 + [page_spec(r) for r in range(n_pg)] + [page_spec(r) for r in range(n_pg)],
        out_specs=pl.BlockSpec((None, rows, dh), lambda b, g, pt: (b, 0, 0)),
        scratch_shapes=[pltpu.VMEM((rows, dh), F32), pltpu.VMEM((rows, 1), F32),
                        pltpu.VMEM((rows, page * n_h), F32)],
    )
    return pl.pallas_call(
        functools.partial(_attn_sample_kernel, n_pg=n_pg),
        grid_spec=grid_spec,
        out_shape=jax.ShapeDtypeStruct((Bs, rows, dh), F32),
        compiler_params=_cparams("parallel", "arbitrary"),
        name=name,
    )(page_table, q, k_new, v_new, bias_col, *([cache_k] * n_pg), *([cache_v] * n_pg))


def _conv_kernel(a_ref, b_ref, ha_ref, hb_ref, w_ref, cb_ref, lg_ref, lb_ref, y_ref, tail_ref,
                 ubuf, cbuf, *, stride, halo, tiles_per_seq, halo_is_state):
    tt = a_ref.shape[0]
    d_b = a_ref.shape[1]
    if halo_is_state:
        ubuf[0:halo, :] = ha_ref[...]
    else:
        first = (pl.program_id(0) % tiles_per_seq) == 0
        uh = ha_ref[...] * jax.nn.sigmoid(hb_ref[...])
        ubuf[0:halo, :] = jnp.where(first, 0.0, uh)
    ubuf[halo:halo + tt, :] = a_ref[...] * jax.nn.sigmoid(b_ref[...])

    base = halo - (CONV_W - 1) * stride
    for c in range(d_b // LANES):
        cols = slice(c * LANES, (c + 1) * LANES)
        acc = jnp.zeros((tt, LANES), F32)
        for j in range(CONV_W):
            s = base + j * stride
            acc = acc + w_ref[j:j + 1, cols] * ubuf[s:s + tt, cols]
        cbuf[:, cols] = acc

    cv = cbuf[...] + cb_ref[...]
    mu = jnp.mean(cv, axis=-1, keepdims=True)
    d = cv - mu
    var = jnp.mean(d * d, axis=-1, keepdims=True)
    y = (d * lax.rsqrt(var + EPS)) * lg_ref[...] + lb_ref[...]
    y_ref[...] = _silu(y).astype(y_ref.dtype)
    n_tail = tail_ref.shape[0]
    tail_ref[...] = ubuf[halo + tt - n_tail:halo + tt, :]


def conv_module(proj, glu_off, d_b, state, conv_w, conv_b, ln_g, ln_b, layer, seq_len, stride, name):
    M = proj.shape[0]
    ca = glu_off // d_b
    w = conv_w[layer]
    vecs = [conv_b[layer].reshape(1, d_b), ln_g[layer].reshape(1, d_b), ln_b[layer].reshape(1, d_b)]
    vec_spec = pl.BlockSpec((1, d_b), lambda i: (0, 0))
    if state is None:
        tt = _pick(seq_len, (256, 128))
        halo = 32
        tps = seq_len // tt
        n_seq = M // seq_len
        hblk = tt // halo
        halo_specs = [pl.BlockSpec((halo, d_b), lambda i: (jnp.maximum(i * hblk - 1, 0), ca)),
                      pl.BlockSpec((halo, d_b), lambda i: (jnp.maximum(i * hblk - 1, 0), ca + 1))]
        halo_args = [proj, proj]
        n_tail = halo
        tail_shape = (n_seq, n_tail, d_b)
        tail_spec = pl.BlockSpec((None, n_tail, d_b), lambda i: (i // tps, 0, 0))
        grid = (M // tt,)
    else:
        tt = M
        halo = state.shape[0]
        tps = 1
        halo_specs = [pl.BlockSpec((halo, d_b), lambda i: (0, 0)),
                      pl.BlockSpec((halo, d_b), lambda i: (0, 0))]
        halo_args = [state, state]
        n_tail = halo
        tail_shape = (n_tail, d_b)
        tail_spec = pl.BlockSpec((n_tail, d_b), lambda i: (0, 0))
        grid = (1,)
    return pl.pallas_call(
        functools.partial(_conv_kernel, stride=stride, halo=halo, tiles_per_seq=tps,
                          halo_is_state=state is not None),
        grid=grid,
        in_specs=[pl.BlockSpec((tt, d_b), lambda i: (i, ca)),
                  pl.BlockSpec((tt, d_b), lambda i: (i, ca + 1))] + halo_specs
                 + [pl.BlockSpec((CONV_W, d_b), lambda i: (0, 0)), vec_spec, vec_spec, vec_spec],
        out_specs=[pl.BlockSpec((tt, d_b), lambda i: (i, 0)), tail_spec],
        out_shape=[jax.ShapeDtypeStruct((M, d_b), BF16), jax.ShapeDtypeStruct(tail_shape, F32)],
        scratch_shapes=[pltpu.VMEM((halo + tt, d_b), F32), pltpu.VMEM((tt, d_b), F32)],
        compiler_params=_cparams("arbitrary"),
        name=name,
    )(proj, proj, *halo_args, w, *vecs)


def _hgrn_lower_bound(lbraw, layer):
    e = jnp.exp(lbraw - jnp.max(lbraw, axis=0, keepdims=True))
    p = e / jnp.sum(e, axis=0, keepdims=True)
    lb = jnp.zeros((1, lbraw.shape[1]), F32)
    for r in range(1, layer + 1):
        lb = lb + p[r:r + 1, :]
    return lb


def _cumsum_rows(x):
    n = x.shape[0]
    r = lax.broadcasted_iota(jnp.int32, (n, n), 0)
    c = lax.broadcasted_iota(jnp.int32, (n, n), 1)
    tri = jnp.where(r >= c, 1.0, 0.0).astype(BF16)
    hi = x.astype(BF16)
    r1 = x - hi.astype(F32)
    mid = r1.astype(BF16)
    lo = (r1 - mid.astype(F32)).astype(BF16)
    parts = jnp.dot(tri, jnp.concatenate([hi, mid, lo], axis=1), preferred_element_type=F32)
    w = x.shape[1]
    return parts[:, :w] + parts[:, w:2 * w] + parts[:, 2 * w:]


def _hgrn_chunk(zf, iv, q, gate, lb, gn, st, *, sub, n_valid):
    c = zf.shape[0]
    log_sig, log_sig_neg = _log_sigmoid_parts(zf)
    a_ = jnp.log(lb)
    b_ = jnp.log1p(-lb) + log_sig
    log_f = jnp.maximum(a_, b_) + jnp.log1p(jnp.exp(-jnp.abs(a_ - b_)))
    k = (1.0 - lb) * jnp.exp(log_sig_neg)
    if n_valid < c:
        valid = lax.broadcasted_iota(jnp.int32, zf.shape, 0) < n_valid
        log_f = jnp.where(valid, log_f, 0.0)
        k = jnp.where(valid, k, 0.0)
    G = _cumsum_rows(log_f)
    st_bf = st.astype(BF16)
    v_bf = iv.astype(BF16)
    row_c = lax.broadcasted_iota(jnp.int32, (c, 1), 0)
    outs = []
    for I in range(c // sub):
        lo, hi = I * sub, (I + 1) * sub
        if lo >= n_valid:
            outs.append(jnp.zeros((sub, iv.shape[1]), F32))
            continue
        qI, GI = q[lo:hi], G[lo:hi]
        o = lax.dot_general((qI * jnp.exp(GI)).astype(BF16), st_bf, (((1,), (1,)), ((), ())),
                            preferred_element_type=F32)
        row_l = lax.broadcasted_iota(jnp.int32, (sub, 1), 0)
        for s in range(min(sub, n_valid - lo)):
            sg = lo + s
            e = jnp.exp(jnp.minimum(GI - G[sg:sg + 1], 0.0))
            col = jnp.sum(qI * (k[sg:sg + 1] * e), axis=-1, keepdims=True)
            o = o + jnp.where(row_l >= s, col, 0.0) * iv[sg:sg + 1]
        if I > 0:
            ref = G[lo - 1:lo]
            qd = qI * jnp.exp(GI - ref)
            kd = jnp.where(row_c < lo, k * jnp.exp(jnp.minimum(ref - G, 0.0)), 0.0)
            A = lax.dot_general(qd.astype(BF16), kd.astype(BF16), (((1,), (1,)), ((), ())),
                                preferred_element_type=F32)
            o = o + jnp.dot(A.astype(BF16), v_bf, preferred_element_type=F32)
        outs.append(o)
    o = jnp.concatenate(outs, axis=0) if len(outs) > 1 else outs[0]
    g_end = G[c - 1:c]
    kdec = k * jnp.exp(g_end - G)
    st_new = st * jnp.exp(g_end) + lax.dot_general(v_bf, kdec.astype(BF16), (((0,), (0,)), ((), ())),
                                                  preferred_element_type=F32)
    r = lax.rsqrt(jnp.mean(o * o, axis=-1, keepdims=True) + EPS)
    y = ((o * r) * gn) * _silu(gate)
    return y, st_new


def _hgrn_prompt_kernel(f_ref, i_ref, q_ref, g_ref, lb_ref, gn_ref, y_ref, s_ref, st_ref, *, layer):
    T = f_ref.shape[0]
    c = HGRN_CHUNK
    lb = _hgrn_lower_bound(lb_ref[...], layer)
    gn = gn_ref[...]
    st_ref[...] = jnp.zeros_like(st_ref)

    def body(ci, carry):
        rows = pl.ds(pl.multiple_of(ci * c, c), c)
        y, st_new = _hgrn_chunk(f_ref[rows, :], i_ref[rows, :], q_ref[rows, :], g_ref[rows, :],
                                lb, gn, st_ref[...], sub=HGRN_SUB, n_valid=c)
        y_ref[rows, :] = y.astype(y_ref.dtype)
        st_ref[...] = st_new
        return carry

    lax.fori_loop(0, T // c, body, 0)
    s_ref[...] = st_ref[...].T


def hgrn_prompt(proj, f_off, lb_raw, norm_g, layer, B, T, name):
    d_c = H_C * DK_C
    fb = f_off // DK_C
    nh = H_C

    def col(k):
        return pl.BlockSpec((T, DK_C), lambda b, h: (b, fb + k * nh + h))

    return pl.pallas_call(
        functools.partial(_hgrn_prompt_kernel, layer=layer),
        grid=(B, H_C),
        in_specs=[col(0), col(1), col(2), col(3),
                  pl.BlockSpec((lb_raw.shape[0], DK_C), lambda b, h: (0, h)),
                  pl.BlockSpec((1, DK_C), lambda b, h: (0, h))],
        out_specs=[pl.BlockSpec((T, DK_C), lambda b, h: (b, h)),
                   pl.BlockSpec((None, None, DK_C, DK_C), lambda b, h: (b, h, 0, 0))],
        out_shape=[jax.ShapeDtypeStruct((B * T, d_c), BF16),
                   jax.ShapeDtypeStruct((B, H_C, DK_C, DK_C), F32)],
        scratch_shapes=[pltpu.VMEM((DK_C, DK_C), F32)],
        compiler_params=_cparams("parallel", "parallel"),
        name=name,
    )(proj, proj, proj, proj, lb_raw, norm_g[layer].reshape(1, d_c))


def _hgrn_sample_kernel(f_ref, i_ref, q_ref, g_ref, lb_ref, gn_ref, s0_ref, y_ref, s_ref, *, layer, ts):
    lb = _hgrn_lower_bound(lb_ref[...], layer)
    y, st_new = _hgrn_chunk(f_ref[...], i_ref[...], q_ref[...], g_ref[...], lb, gn_ref[...],
                            s0_ref[...].T, sub=f_ref.shape[0], n_valid=ts)
    y_ref[...] = y.astype(y_ref.dtype)
    s_ref[...] = st_new.T


def hgrn_sample(fiqg, state, lb_raw, norm_g, layer, ts, name):
    _, Bs, tp, d_c = fiqg.shape

    def col(k):
        return pl.BlockSpec((None, None, tp, DK_C), lambda b, h: (k, b, 0, h))

    st_spec = pl.BlockSpec((None, None, DK_C, DK_C), lambda b, h: (b, h, 0, 0))
    return pl.pallas_call(
        functools.partial(_hgrn_sample_kernel, layer=layer, ts=ts),
        grid=(Bs, H_C),
        in_specs=[col(0), col(1), col(2), col(3),
                  pl.BlockSpec((lb_raw.shape[0], DK_C), lambda b, h: (0, h)),
                  pl.BlockSpec((1, DK_C), lambda b, h: (0, h)),
                  st_spec],
        out_specs=[pl.BlockSpec((None, tp, DK_C), lambda b, h: (b, 0, h)), st_spec],
        out_shape=[jax.ShapeDtypeStruct((Bs, tp, d_c), BF16),
                   jax.ShapeDtypeStruct(state.shape, F32)],
        compiler_params=_cparams("parallel", "parallel"),
        name=name,
    )(fiqg, fiqg, fiqg, fiqg, lb_raw, norm_g[layer].reshape(1, d_c), state)


def _ffn_act_kernel(g_ref, u_ref, h_ref, w_ref, o_ref, gbuf, *, stride, halo, tiles_per_seq, halo_is_state):
    tt = g_ref.shape[0]
    if halo_is_state:
        gbuf[0:halo, :] = h_ref[...]
    else:
        first = (pl.program_id(1) % tiles_per_seq) == 0
        gbuf[0:halo, :] = jnp.where(first, 0.0, h_ref[...])
    gbuf[halo:halo + tt, :] = g_ref[...]
    a = w_ref[FFN_CONV_W - 1:FFN_CONV_W, :] * g_ref[...]
    for j in range(FFN_CONV_W - 1):
        s = halo - (FFN_CONV_W - 1 - j) * stride
        a = a + w_ref[j:j + 1, :] * gbuf[s:s + tt, :]
    o_ref[...] = (_silu(a) * u_ref[...]).astype(o_ref.dtype)


def ffn_act(G, U, state, conv_w, layer, seq_len, stride, name):
    M, d_ff = G.shape
    tc = _pick(d_ff, (1408, 512, 256, 128))
    w = conv_w[layer]
    if state is None:
        tt = _pick(seq_len, (512, 256, 128))
        halo = SUBLANES
        tps = seq_len // tt
        hblk = tt // halo
        halo_spec = pl.BlockSpec((halo, tc), lambda c, i: (jnp.maximum(i * hblk - 1, 0), c))
        halo_arg = G
    else:
        tt = M
        halo = state.shape[0]
        tps = 1
        halo_spec = pl.BlockSpec((halo, tc), lambda c, i: (0, c))
        halo_arg = state
    return pl.pallas_call(
        functools.partial(_ffn_act_kernel, stride=stride, halo=halo, tiles_per_seq=tps,
                          halo_is_state=state is not None),
        grid=(d_ff // tc, M // tt),
        in_specs=[pl.BlockSpec((tt, tc), lambda c, i: (i, c)),
                  pl.BlockSpec((tt, tc), lambda c, i: (i, c)),
                  halo_spec,
                  pl.BlockSpec((FFN_CONV_W, tc), lambda c, i: (0, c))],
        out_specs=pl.BlockSpec((tt, tc), lambda c, i: (i, c)),
        out_shape=jax.ShapeDtypeStruct((M, d_ff), BF16),
        scratch_shapes=[pltpu.VMEM((halo + tt, tc), F32)],
        compiler_params=_cparams("parallel", "parallel"),
        name=name,
    )(G, U, halo_arg, w)


def kernel(x_prompt, x_sample, cache_k, cache_v, state_conv, state_hgrn, state_ffn_conv, page_table, hgrn_lower_bound, sb_bias, norm_mix_g, w_in, conv_w, conv_b, conv_ln_g, conv_ln_b, hgrn_norm_g, w_out_a, w_out_b, w_out_c, w_o, norm_ffn_g, w_gate, w_up, ffn_conv_w, w_down, norm_final_g):
    Bp, T, D = x_prompt.shape
    Bs, Ts, _ = x_sample.shape
    depth = w_in.shape[0]
    d_a = H_A * DH_A
    d_b = conv_w.shape[2]
    d_c = H_C * DK_C
    d_ff = w_gate.shape[2]
    off_glu = 3 * d_a
    off_f = off_glu + 2 * d_b
    off_gates = off_f + 4 * d_c
    tp = 16

    xp =x_prompt.reshape(Bp * T, D)
    xs = x_sample.transpose(1, 0, 2).reshape(Ts * Bs, D)

    def to_batch_major(a):
        return a.reshape(Ts, Bs, a.shape[-1]).transpose(1, 0, 2)

    def to_time_major(a):
        return a.transpose(1, 0, 2).reshape(Ts * Bs, a.shape[-1])

    outs = {k: [] for k in ("kp", "vp", "ks", "vs", "cp", "cs", "hp", "hs", "fp", "fs")}
    for l in range(depth):
        tag = f"l{l}"
        (pp,) = norm_matmul(xp, norm_mix_g, [w_in], l, f"in_p_{tag}")
        oa = attn_prompt(pp, sb_bias, l, Bp, T, f"attn_p_{tag}")
        ob, tail = conv_module(pp, off_glu, d_b, None, conv_w, conv_b, conv_ln_g, conv_ln_b, l, T, 1,
                               f"conv_p_{tag}")
        oc, hst = hgrn_prompt(pp, off_f, hgrn_lower_bound, hgrn_norm_g, l, Bp, T, f"hgrn_p_{tag}")
        mg = merge_branches(oa, ob, oc, pp, off_gates, w_out_a, w_out_b, w_out_c, l, f"merge_p_{tag}")
        xp = matmul_residual(xp, mg, w_o, l, f"wo_p_{tag}")
        gp, up = norm_matmul(xp, norm_ffn_g, [w_gate, w_up], l, f"ffn_in_p_{tag}")
        act = ffn_act(gp, up, None, ffn_conv_w, l, T, 1, f"ffn_act_p_{tag}")
        xp = matmul_residual(xp, act, w_down, l, f"down_p_{tag}")
        outs["kp"].append(pp[:, d_a:2 * d_a].reshape(Bp, T, H_A, DH_A))
        outs["vp"].append(pp[:, 2 * d_a:3 * d_a].reshape(Bp, T, H_A, DH_A))
        outs["cp"].append(tail[:, tail.shape[1] - (CONV_W - 1):, :])
        outs["hp"].append(hst)
        outs["fp"].append(gp.reshape(Bp, T, d_ff)[:, T - (FFN_CONV_W - 1):, :])

        (ps,) = norm_matmul(xs, norm_mix_g, [w_in], l, f"in_s_{tag}")
        qkv = to_batch_major(ps[:, :3 * d_a])
        q_s, k_s, v_s = qkv[..., :d_a], qkv[..., d_a:2 * d_a], qkv[..., 2 * d_a:]
        exp_rows = lambda a: a.reshape(Bs, Ts * H_A, DH_A)
        oa = attn_sample(exp_rows(q_s), exp_rows(k_s), exp_rows(v_s), cache_k, cache_v, page_table,
                         sb_bias, l, f"attn_s_{tag}")
        oa = to_time_major(oa.reshape(Bs, Ts, d_a)).astype(BF16)
        cst =state_conv[l].transpose(1, 0, 2).reshape((CONV_W - 1) * Bs, d_b)
        ob, tail = conv_module(ps, off_glu, d_b, cst, conv_w, conv_b, conv_ln_g, conv_ln_b, l, Ts, Bs,
                               f"conv_s_{tag}")
        fiqg = to_batch_major(ps[:, off_f:off_f + 4 * d_c]).reshape(Bs, Ts, 4, d_c)
        fiqg = jnp.pad(fiqg.transpose(2, 0, 1, 3), ((0, 0), (0, 0), (0, tp - Ts), (0, 0)))
        oc, hst = hgrn_sample(fiqg, state_hgrn[l], hgrn_lower_bound, hgrn_norm_g, l, Ts, f"hgrn_s_{tag}")
        oc = to_time_major(oc[:, :Ts, :])
        mg = merge_branches(oa, ob, oc, ps, off_gates, w_out_a, w_out_b, w_out_c, l, f"merge_s_{tag}")
        xs = matmul_residual(xs, mg, w_o, l, f"wo_s_{tag}")
        gs, us = norm_matmul(xs, norm_ffn_g, [w_gate, w_up], l, f"ffn_in_s_{tag}")
        fst = state_ffn_conv[l].transpose(1, 0, 2).reshape((FFN_CONV_W - 1) * Bs, d_ff)
        act = ffn_act(gs, us, fst, ffn_conv_w, l, Ts, Bs, f"ffn_act_s_{tag}")
        xs = matmul_residual(xs, act, w_down, l, f"down_s_{tag}")
        outs["ks"].append(k_s.reshape(Bs, Ts, H_A, DH_A))
        outs["vs"].append(v_s.reshape(Bs, Ts, H_A, DH_A))
        outs["cs"].append(tail.reshape(CONV_W - 1, Bs, d_b).transpose(1, 0, 2))
        outs["hs"].append(hst)
        ext_g = jnp.concatenate([fst, gs], axis=0)
        outs["fs"].append(ext_g[ext_g.shape[0] - (FFN_CONV_W - 1) * Bs:]
                          .reshape(FFN_CONV_W - 1, Bs, d_ff).transpose(1, 0, 2))

    y_p = rmsnorm_rows(xp, norm_final_g, "final_p").reshape(Bp, T, D)
    y_s = to_batch_major(rmsnorm_rows(xs, norm_final_g, "final_s"))
    st = lambda k: jnp.stack(outs[k])
    return (y_p, y_s, st("kp"), st("vp"), st("ks"), st("vs"), st("cp"), st("cs"),
            st("hp"), st("hs"), st("fp"), st("fs"))
```

```python
import functools

import jax
import jax.numpy as jnp
from jax import lax
from jax.experimental import pallas as pl
from jax.experimental.pallas import tpu as pltpu

F32 = jnp.float32
BF16 = jnp.bfloat16
EPS = 1e-6

VMEM_LIMIT_BYTES = 56 * 1024 * 1024
SUBLANES = 8
LANES = 128

H_A = 16
DH_A = 64
H_C = 8
DK_C = 128
CONV_W = 31
FFN_CONV_W = 3
HGRN_CHUNK = 64
HGRN_SUB = 16
ATT_TQ = 256
ATT_TK = 256
ATT_HEADS_PER_STEP = 4


def _cparams(*sem):
    return pltpu.CompilerParams(dimension_semantics=sem, vmem_limit_bytes=VMEM_LIMIT_BYTES)


def _pick(n, prefs):
    for p in prefs:
        if n % p == 0:
            return p
    return n


def _log_sigmoid_parts(z):
    l = jnp.log1p(jnp.exp(-jnp.abs(z)))
    return jnp.minimum(z, 0.0) - l, -(jnp.maximum(z, 0.0) + l)


def _silu(x):
    return x * jax.nn.sigmoid(x)


def _norm_mm_kernel(x_ref, g_ref, *refs, n_w):
    w_refs = refs[:n_w]
    o_refs = refs[n_w:2 * n_w]
    h_ref = refs[2 * n_w]

    @pl.when(pl.program_id(1) == 0)
    def _():
        xf = x_ref[...]
        r = lax.rsqrt(jnp.mean(xf * xf, axis=-1, keepdims=True) + EPS)
        h_ref[...] = ((xf * r) * g_ref[...]).astype(BF16)

    h = h_ref[...]
    for w_ref, o_ref in zip(w_refs, o_refs):
        o_ref[...] = jnp.dot(h, w_ref[...].astype(BF16), preferred_element_type=F32)


def norm_matmul(x, g, ws, layer, name):
    M, D = x.shape
    N = ws[0].shape[2]
    tm = _pick(M, (1024, 512, 256, 128))
    tn = _pick(N, (512, 256, 128))
    n_w = len(ws)
    g3 = g.reshape(g.shape[0], 1, D)
    outs = pl.pallas_call(
        functools.partial(_norm_mm_kernel, n_w=n_w),
        grid=(M // tm, N // tn),
        in_specs=[pl.BlockSpec((tm, D), lambda i, j: (i, 0)),
                  pl.BlockSpec((None, 1, D), lambda i, j: (layer, 0, 0))]
                 + [pl.BlockSpec((None, D, tn), lambda i, j: (layer, 0, j)) for _ in ws],
        out_specs=[pl.BlockSpec((tm, tn), lambda i, j: (i, j)) for _ in ws],
        out_shape=[jax.ShapeDtypeStruct((M, N), F32) for _ in ws],
        scratch_shapes=[pltpu.VMEM((tm, D), BF16)],
        compiler_params=_cparams("parallel", "arbitrary"),
        name=name,
    )(x, g3, *ws)
    return outs


def _rmsnorm_kernel(x_ref, g_ref, o_ref):
    xf = x_ref[...]
    r = lax.rsqrt(jnp.mean(xf * xf, axis=-1, keepdims=True) + EPS)
    o_ref[...] = (xf * r) * g_ref[...]


def rmsnorm_rows(x, g, name):
    M, D = x.shape
    tm = _pick(M, (512, 256, 128))
    return pl.pallas_call(
        _rmsnorm_kernel,
        grid=(M // tm,),
        in_specs=[pl.BlockSpec((tm, D), lambda i: (i, 0)),
                  pl.BlockSpec((1, D), lambda i: (0, 0))],
        out_specs=pl.BlockSpec((tm, D), lambda i: (i, 0)),
        out_shape=jax.ShapeDtypeStruct((M, D), F32),
        compiler_params=_cparams("parallel"),
        name=name,
    )(x, g.reshape(1, D))


def _mm_res_kernel(x_ref, a_ref, w_ref, o_ref):
    o_ref[...] = x_ref[...] + jnp.dot(a_ref[...], w_ref[...].astype(BF16),
                                      preferred_element_type=F32)


def matmul_residual(x, a, w, layer, name):
    M, N = x.shape
    K = a.shape[1]
    tm = _pick(M, (1024, 512, 256, 128))
    tn = 256 if K > 4096 else _pick(N, (512, 256, 128))
    return pl.pallas_call(
        _mm_res_kernel,
        grid=(M // tm, N // tn),
        in_specs=[pl.BlockSpec((tm, tn), lambda i, j: (i, j)),
                  pl.BlockSpec((tm, K), lambda i, j: (i, 0)),
                  pl.BlockSpec((None, K, tn), lambda i, j: (layer, 0, j))],
        out_specs=pl.BlockSpec((tm, tn), lambda i, j: (i, j)),
        out_shape=jax.ShapeDtypeStruct((M, N), F32),
        compiler_params=_cparams("parallel", "arbitrary"),
        name=name,
    )(x, a, w)


def _merge_kernel(oa_ref, ob_ref, oc_ref, g0_ref, g1_ref, g2_ref, wa_ref, wb_ref, wc_ref, o_ref):
    def branch(o_r, g_r, w_r):
        y = jnp.dot(o_r[...], w_r[...].astype(BF16), preferred_element_type=F32)
        return jax.nn.sigmoid(g_r[...]) * y

    m = branch(oa_ref, g0_ref, wa_ref) + branch(ob_ref, g1_ref, wb_ref) + branch(oc_ref, g2_ref, wc_ref)
    o_ref[...] = m.astype(BF16)


def merge_branches(oa, ob, oc, proj, gate_off, wa, wb, wc, layer, name):
    M, K = oa.shape
    N = wa.shape[2]
    tm = _pick(M, (1024, 512, 256, 128))
    tn = _pick(N, (512, 256, 128))
    gb = gate_off // tn
    nb = N // tn
    o_spec = pl.BlockSpec((tm, K), lambda i, j: (i, 0))
    w_spec = pl.BlockSpec((None, K, tn), lambda i, j: (layer, 0, j))
    return pl.pallas_call(
        _merge_kernel,
        grid=(M // tm, nb),
        in_specs=[o_spec, o_spec, o_spec,
                  pl.BlockSpec((tm, tn), lambda i, j: (i, gb + j)),
                  pl.BlockSpec((tm, tn), lambda i, j: (i, gb + nb + j)),
                  pl.BlockSpec((tm, tn), lambda i, j: (i, gb + 2 * nb + j)),
                  w_spec, w_spec, w_spec],
        out_specs=pl.BlockSpec((tm, tn), lambda i, j: (i, j)),
        out_shape=jax.ShapeDtypeStruct((M, N), BF16),
        compiler_params=_cparams("parallel", "arbitrary"),
        name=name,
    )(oa, ob, oc, proj, proj, proj, wa, wb, wc)


LOG2E = 1.4426950408889634


def _strict_upper2(n):
    r = lax.broadcasted_iota(jnp.int32, (2 * n, n), 0)
    c = lax.broadcasted_iota(jnp.int32, (2 * n, n), 1)
    r = jnp.where(r >= n, r - n, r)
    return jnp.where(r > c, 1.0, 0.0).astype(BF16)


def _neg_abs(x):
    bits = pltpu.bitcast(x, jnp.uint32) | jnp.uint32(0x80000000)
    return pltpu.bitcast(bits, F32)


MASKED_LOGIT = -1e30


def _sb_tiles(z2s, v_bfs, u2, crys, chained=False, v_transposed=False):
    staged = []
    for z2 in z2s:
        l2 = jnp.log2(1.0 + jnp.exp2(_neg_abs(z2)))
        log_beta = jnp.minimum(z2, 0.0) - l2
        neg_log_rem = jnp.maximum(z2, 0.0) + l2
        hi = neg_log_rem.astype(BF16)
        lo = (neg_log_rem - hi.astype(F32)).astype(BF16)
        suffix = jnp.dot(jnp.concatenate([hi, lo], axis=1), u2, preferred_element_type=F32)
        staged.append((log_beta, suffix, jnp.sum(neg_log_rem, axis=-1, keepdims=True)))
    contribs, new_crys = [], []
    for n, ((log_beta, suffix, row_sum), v_bf) in enumerate(zip(staged, v_bfs)):
        cry = crys if chained else crys[n]
        w = jnp.exp2(log_beta - suffix - cry).astype(BF16)
        v_contract = 1 if v_transposed else 0
        contribs.append(lax.dot_general(w, v_bf, (((1,), (v_contract,)), ((), ())),
                                        preferred_element_type=F32))
        if chained:
            crys = cry + row_sum
        else:
            new_crys.append(cry + row_sum)
    return contribs, (crys if chained else new_crys)


def _attn_prompt_kernel(bias_ref, q_ref, k_ref, v_ref, o_ref, kb_ref, vb_ref, acc_ref, cry_ref, *,
                        layer, heads_per_blk):
    hp = pl.program_id(1)
    i = pl.program_id(2)
    tq = q_ref.shape[0]
    tk = ATT_TK
    dh = DH_A

    heads = range(heads_per_blk)

    @pl.when(i == 0)
    def _():
        lane = lax.broadcasted_iota(jnp.int32, (k_ref.shape[0], dh), 1)
        for hh in heads:
            b2 = jnp.full(lane.shape, bias_ref[layer * H_A + hp * heads_per_blk + hh] * LOG2E, F32)
            b_hi = b2.astype(BF16).astype(F32)
            extra = jnp.where(lane == 0, b_hi, jnp.where(lane == 1, b2 - b_hi, 0.0)).astype(BF16)
            kb_ref[hh] = jnp.concatenate([k_ref[:, hh * dh:(hh + 1) * dh].astype(BF16), extra], axis=1)
            vb_ref[hh] = v_ref[:, hh * dh:(hh + 1) * dh].astype(BF16)

    u2 = _strict_upper2(tk)
    diag_mask = (lax.broadcasted_iota(jnp.int32, (tq, tk), 1)
                 < lax.broadcasted_iota(jnp.int32, (tq, tk), 0))
    ones2 = jnp.where(lax.broadcasted_iota(jnp.int32, (tq, dh), 1) < 2, 1.0, 0.0).astype(BF16)
    q2 = [jnp.concatenate([(q_ref[:, hh * dh:(hh + 1) * dh] * (dh ** -0.5 * LOG2E)).astype(BF16), ones2],
                          axis=1) for hh in heads]

    def tile(j, mask, first):
        rows = pl.ds(pl.multiple_of(j * tk, tk), tk)
        z2s = [lax.dot_general(q2[hh], kb_ref[hh, rows, :], (((1,), (1,)), ((), ())),
                               preferred_element_type=F32) for hh in heads]
        if mask is not None:
            z2s = [jnp.where(mask, z2, MASKED_LOGIT) for z2 in z2s]
        crys = [jnp.zeros((tq, 1), F32) if first else cry_ref[hh] for hh in heads]
        contribs, crys = _sb_tiles(z2s, [vb_ref[hh, rows, :] for hh in heads], u2, crys)
        for hh in heads:
            acc_ref[hh] = contribs[hh] if first else acc_ref[hh] + contribs[hh]
            cry_ref[hh] = crys[hh]

    tile(i, diag_mask, True)

    def body(jj, carry):
        tile(i - 1 - jj, None, False)
        return carry

    lax.fori_loop(0, i, body, 0)
    for hh in heads:
        o_ref[:, hh * dh:(hh + 1) * dh] = acc_ref[hh].astype(o_ref.dtype)


def attn_prompt(proj, sb_bias, layer, B, T, name):
    d_a = H_A * DH_A
    hpb = ATT_HEADS_PER_STEP
    wb = hpb * DH_A
    n_hp = H_A // hpb
    nq = T // ATT_TQ
    kv_blk = d_a // wb
    return pl.pallas_call(
        functools.partial(_attn_prompt_kernel, layer=layer, heads_per_blk=hpb),
        grid=(B, n_hp, nq),
        in_specs=[pl.BlockSpec(memory_space=pltpu.SMEM),
                  pl.BlockSpec((ATT_TQ, wb), lambda b, h, i: (b * nq + i, h)),
                  pl.BlockSpec((T, wb), lambda b, h, i: (b, kv_blk + h)),
                  pl.BlockSpec((T, wb), lambda b, h, i: (b, 2 * kv_blk + h))],
        out_specs=pl.BlockSpec((ATT_TQ, wb), lambda b, h, i: (b * nq + i, h)),
        out_shape=jax.ShapeDtypeStruct((B * T, d_a), BF16),
        scratch_shapes=[pltpu.VMEM((hpb, T, 2 * DH_A), BF16), pltpu.VMEM((hpb, T, DH_A), BF16),
                        pltpu.VMEM((hpb, ATT_TQ, DH_A), F32), pltpu.VMEM((hpb, ATT_TQ, 1), F32)],
        compiler_params=_cparams("parallel", "parallel", "arbitrary"),
        name=name,
    )(sb_bias.reshape(-1), proj, proj, proj)


def _attn_sample_kernel(pt_ref, q_ref, kn_ref, vn_ref, bias_ref, *refs, n_pg):
    del pt_ref
    k_refs = refs[:n_pg]
    v_refs = refs[n_pg:2 * n_pg]
    o_ref = refs[2 * n_pg]
    qbd_ref, acc_ref, cry_ref = refs[2 * n_pg + 1:]
    g = pl.program_id(1)
    ts, d_a = q_ref.shape
    rows = ts * H_A
    page = k_refs[0].shape[1]
    head_mask = (lax.broadcasted_iota(jnp.int32, (H_A, d_a), 1) // DH_A
                 == lax.broadcasted_iota(jnp.int32, (H_A, d_a), 0))
    bias2 = bias_ref[...] * LOG2E
    u2 = _strict_upper2(page)

    def logits(kt_ref):
        return jnp.dot(qbd_ref[...], kt_ref[...].astype(BF16), preferred_element_type=F32) + bias2

    @pl.when(g == 0)
    def _():
        q = q_ref[...] * (DH_A ** -0.5 * LOG2E)
        qbd_ref[...] = jnp.concatenate(
            [jnp.where(head_mask, jnp.broadcast_to(q[t:t + 1, :], (H_A, d_a)), 0.0) for t in range(ts)],
            axis=0).astype(BF16)
        t_of_row = lax.broadcasted_iota(jnp.int32, (rows, page), 0) // H_A
        mask = lax.broadcasted_iota(jnp.int32, (rows, page), 1) < t_of_row
        contribs, crys = _sb_tiles([jnp.where(mask, logits(kn_ref), MASKED_LOGIT)],
                                   [vn_ref[...].astype(BF16)], u2, [jnp.zeros((rows, 1), F32)],
                                   v_transposed=True)
        acc_ref[...] = contribs[0]
        cry_ref[...] = crys[0]

    contribs, cry = _sb_tiles([logits(k_refs[r]) for r in range(n_pg)],
                              [v_refs[r][...].astype(BF16) for r in range(n_pg)],
                              u2, cry_ref[...], chained=True, v_transposed=True)
    acc = acc_ref[...]
    for contrib in contribs:
        acc = acc + contrib
    acc_ref[...] = acc
    cry_ref[...] = cry

    @pl.when(g == pl.num_programs(1) - 1)
    def _():
        hm = head_mask.astype(F32)
        for t in range(ts):
            o_ref[t:t + 1, :] = jnp.sum(acc_ref[t * H_A:(t + 1) * H_A, :] * hm, axis=0, keepdims=True)


def attn_sample(q, k_new_t, v_new_t, cache_kt, cache_vt, page_table, sb_bias, layer, name):
    Bs, Ts, d_a = q.shape
    n_pages = page_table.shape[1]
    page = cache_kt.shape[3]
    n_pg = _pick(n_pages, (8, 4, 2, 1))
    n_grp = n_pages // n_pg
    bias_col = jnp.tile(sb_bias[layer], Ts).reshape(Ts * H_A, 1)

    def page_spec(r):
        return pl.BlockSpec((None, None, d_a, page),
                            lambda b, g, pt: (layer, pt[b, n_pages - 1 - (g * n_pg + r)], 0, 0))

    new_spec = pl.BlockSpec((None, d_a, page), lambda b, g, pt: (b, 0, 0))
    grid_spec = pltpu.PrefetchScalarGridSpec(
        num_scalar_prefetch=1,
        grid=(Bs, n_grp),
        in_specs=[pl.BlockSpec((None, Ts, d_a), lambda b, g, pt: (b, 0, 0)), new_spec, new_spec,
                  pl.BlockSpec((Ts * H_A, 1), lambda b, g, pt: (0, 0))]
                 + [page_spec(r) for r in range(n_pg)] + [page_spec(r) for r in range(n_pg)],
        out_specs=pl.BlockSpec((None, Ts, d_a), lambda b, g, pt: (b, 0, 0)),
        scratch_shapes=[pltpu.VMEM((Ts * H_A, d_a), BF16),
                        pltpu.VMEM((Ts * H_A, d_a), F32),
                        pltpu.VMEM((Ts * H_A, 1), F32)],
    )
    return pl.pallas_call(
        functools.partial(_attn_sample_kernel, n_pg=n_pg),
        grid_spec=grid_spec,
        out_shape=jax.ShapeDtypeStruct((Bs, Ts, d_a), F32),
        compiler_params=_cparams("parallel", "arbitrary"),
        name=name,
    )(page_table, q, k_new_t, v_new_t, bias_col, *([cache_kt] * n_pg), *([cache_vt] * n_pg))


def _conv_kernel(a_ref, b_ref, ha_ref, hb_ref, w_ref, cb_ref, lg_ref, lb_ref, y_ref, tail_ref,
                 ubuf, cbuf, *, stride, halo, tiles_per_seq, halo_is_state):
    tt = a_ref.shape[0]
    d_b = a_ref.shape[1]
    if halo_is_state:
        ubuf[0:halo, :] = ha_ref[...]
    else:
        first = (pl.program_id(0) % tiles_per_seq) == 0
        uh = ha_ref[...] * jax.nn.sigmoid(hb_ref[...])
        ubuf[0:halo, :] = jnp.where(first, 0.0, uh)
    ubuf[halo:halo + tt, :] = a_ref[...] * jax.nn.sigmoid(b_ref[...])

    base = halo - (CONV_W - 1) * stride
    for c in range(d_b // LANES):
        cols = slice(c * LANES, (c + 1) * LANES)
        acc = jnp.zeros((tt, LANES), F32)
        for j in range(CONV_W):
            s = base + j * stride
            acc = acc + w_ref[j:j + 1, cols] * ubuf[s:s + tt, cols]
        cbuf[:, cols] = acc

    cv = cbuf[...] + cb_ref[...]
    mu = jnp.mean(cv, axis=-1, keepdims=True)
    d = cv - mu
    var = jnp.mean(d * d, axis=-1, keepdims=True)
    y = (d * lax.rsqrt(var + EPS)) * lg_ref[...] + lb_ref[...]
    y_ref[...] = _silu(y).astype(y_ref.dtype)
    n_tail = tail_ref.shape[0]
    tail_ref[...] = ubuf[halo + tt - n_tail:halo + tt, :]


def conv_module(proj, glu_off, d_b, state, conv_w, conv_b, ln_g, ln_b, layer, seq_len, stride, name):
    M = proj.shape[0]
    ca = glu_off // d_b
    w = conv_w[layer]
    vecs = [conv_b[layer].reshape(1, d_b), ln_g[layer].reshape(1, d_b), ln_b[layer].reshape(1, d_b)]
    vec_spec = pl.BlockSpec((1, d_b), lambda i: (0, 0))
    if state is None:
        tt = _pick(seq_len, (256, 128))
        halo = 32
        tps = seq_len // tt
        n_seq = M // seq_len
        hblk = tt // halo
        halo_specs = [pl.BlockSpec((halo, d_b), lambda i: (jnp.maximum(i * hblk - 1, 0), ca)),
                      pl.BlockSpec((halo, d_b), lambda i: (jnp.maximum(i * hblk - 1, 0), ca + 1))]
        halo_args = [proj, proj]
        n_tail = halo
        tail_shape = (n_seq, n_tail, d_b)
        tail_spec = pl.BlockSpec((None, n_tail, d_b), lambda i: (i // tps, 0, 0))
        grid = (M // tt,)
    else:
        tt = M
        halo = state.shape[0]
        tps = 1
        halo_specs = [pl.BlockSpec((halo, d_b), lambda i: (0, 0)),
                      pl.BlockSpec((halo, d_b), lambda i: (0, 0))]
        halo_args = [state, state]
        n_tail = halo
        tail_shape = (n_tail, d_b)
        tail_spec = pl.BlockSpec((n_tail, d_b), lambda i: (0, 0))
        grid = (1,)
    return pl.pallas_call(
        functools.partial(_conv_kernel, stride=stride, halo=halo, tiles_per_seq=tps,
                          halo_is_state=state is not None),
        grid=grid,
        in_specs=[pl.BlockSpec((tt, d_b), lambda i: (i, ca)),
                  pl.BlockSpec((tt, d_b), lambda i: (i, ca + 1))] + halo_specs
                 + [pl.BlockSpec((CONV_W, d_b), lambda i: (0, 0)), vec_spec, vec_spec, vec_spec],
        out_specs=[pl.BlockSpec((tt, d_b), lambda i: (i, 0)), tail_spec],
        out_shape=[jax.ShapeDtypeStruct((M, d_b), BF16), jax.ShapeDtypeStruct(tail_shape, F32)],
        scratch_shapes=[pltpu.VMEM((halo + tt, d_b), F32), pltpu.VMEM((tt, d_b), F32)],
        compiler_params=_cparams("arbitrary"),
        name=name,
    )(proj, proj, *halo_args, w, *vecs)


def _hgrn_lower_bound(lbraw, layer):
    e = jnp.exp(lbraw - jnp.max(lbraw, axis=0, keepdims=True))
    p = e / jnp.sum(e, axis=0, keepdims=True)
    lb = jnp.zeros((1, lbraw.shape[1]), F32)
    for r in range(1, layer + 1):
        lb = lb + p[r:r + 1, :]
    return lb


def _cumsum_rows(x):
    n = x.shape[0]
    r = lax.broadcasted_iota(jnp.int32, (n, n), 0)
    c = lax.broadcasted_iota(jnp.int32, (n, n), 1)
    tri = jnp.where(r >= c, 1.0, 0.0).astype(BF16)
    hi = x.astype(BF16)
    r1 = x - hi.astype(F32)
    mid = r1.astype(BF16)
    lo = (r1 - mid.astype(F32)).astype(BF16)
    parts = jnp.dot(tri, jnp.concatenate([hi, mid, lo], axis=1), preferred_element_type=F32)
    w = x.shape[1]
    return parts[:, :w] + parts[:, w:2 * w] + parts[:, 2 * w:]


def _hgrn_chunk(zf, iv, q, gate, lb, gn, st, *, sub, n_valid):
    c = zf.shape[0]
    log_sig, log_sig_neg = _log_sigmoid_parts(zf)
    a_ = jnp.log(lb)
    b_ = jnp.log1p(-lb) + log_sig
    log_f = jnp.maximum(a_, b_) + jnp.log1p(jnp.exp(-jnp.abs(a_ - b_)))
    k = (1.0 - lb) * jnp.exp(log_sig_neg)
    if n_valid < c:
        valid = lax.broadcasted_iota(jnp.int32, zf.shape, 0) < n_valid
        log_f = jnp.where(valid, log_f, 0.0)
        k = jnp.where(valid, k, 0.0)
    G = _cumsum_rows(log_f)
    st_bf = st.astype(BF16)
    v_bf = iv.astype(BF16)
    row_c = lax.broadcasted_iota(jnp.int32, (c, 1), 0)
    outs = []
    for I in range(c // sub):
        lo, hi = I * sub, (I + 1) * sub
        if lo >= n_valid:
            outs.append(jnp.zeros((sub, iv.shape[1]), F32))
            continue
        qI, GI = q[lo:hi], G[lo:hi]
        o = lax.dot_general((qI * jnp.exp(GI)).astype(BF16), st_bf, (((1,), (1,)), ((), ())),
                            preferred_element_type=F32)
        row_l = lax.broadcasted_iota(jnp.int32, (sub, 1), 0)
        for s in range(min(sub, n_valid - lo)):
            sg = lo + s
            e = jnp.exp(jnp.minimum(GI - G[sg:sg + 1], 0.0))
            col = jnp.sum(qI * (k[sg:sg + 1] * e), axis=-1, keepdims=True)
            o = o + jnp.where(row_l >= s, col, 0.0) * iv[sg:sg + 1]
        if I > 0:
            ref = G[lo - 1:lo]
            qd = qI * jnp.exp(GI - ref)
            kd = jnp.where(row_c < lo, k * jnp.exp(jnp.minimum(ref - G, 0.0)), 0.0)
            A = lax.dot_general(qd.astype(BF16), kd.astype(BF16), (((1,), (1,)), ((), ())),
                                preferred_element_type=F32)
            o = o + jnp.dot(A.astype(BF16), v_bf, preferred_element_type=F32)
        outs.append(o)
    o = jnp.concatenate(outs, axis=0) if len(outs) > 1 else outs[0]
    g_end = G[c - 1:c]
    kdec = k * jnp.exp(g_end - G)
    st_new = st * jnp.exp(g_end) + lax.dot_general(v_bf, kdec.astype(BF16), (((0,), (0,)), ((), ())),
                                                  preferred_element_type=F32)
    r = lax.rsqrt(jnp.mean(o * o, axis=-1, keepdims=True) + EPS)
    y = ((o * r) * gn) * _silu(gate)
    return y, st_new


def _hgrn_prompt_kernel(f_ref, i_ref, q_ref, g_ref, lb_ref, gn_ref, y_ref, s_ref, st_ref, *, layer):
    T = f_ref.shape[0]
    c = HGRN_CHUNK
    lb = _hgrn_lower_bound(lb_ref[...], layer)
    gn = gn_ref[...]
    st_ref[...] = jnp.zeros_like(st_ref)

    def body(ci, carry):
        rows = pl.ds(pl.multiple_of(ci * c, c), c)
        y, st_new = _hgrn_chunk(f_ref[rows, :], i_ref[rows, :], q_ref[rows, :], g_ref[rows, :],
                                lb, gn, st_ref[...], sub=HGRN_SUB, n_valid=c)
        y_ref[rows, :] = y.astype(y_ref.dtype)
        st_ref[...] = st_new
        return carry

    lax.fori_loop(0, T // c, body, 0)
    s_ref[...] = st_ref[...].T


def hgrn_prompt(proj, f_off, lb_raw, norm_g, layer, B, T, name):
    d_c = H_C * DK_C
    fb = f_off // DK_C
    nh = H_C

    def col(k):
        return pl.BlockSpec((T, DK_C), lambda b, h: (b, fb + k * nh + h))

    return pl.pallas_call(
        functools.partial(_hgrn_prompt_kernel, layer=layer),
        grid=(B, H_C),
        in_specs=[col(0), col(1), col(2), col(3),
                  pl.BlockSpec((lb_raw.shape[0], DK_C), lambda b, h: (0, h)),
                  pl.BlockSpec((1, DK_C), lambda b, h: (0, h))],
        out_specs=[pl.BlockSpec((T, DK_C), lambda b, h: (b, h)),
                   pl.BlockSpec((None, None, DK_C, DK_C), lambda b, h: (b, h, 0, 0))],
        out_shape=[jax.ShapeDtypeStruct((B * T, d_c), BF16),
                   jax.ShapeDtypeStruct((B, H_C, DK_C, DK_C), F32)],
        scratch_shapes=[pltpu.VMEM((DK_C, DK_C), F32)],
        compiler_params=_cparams("parallel", "parallel"),
        name=name,
    )(proj, proj, proj, proj, lb_raw, norm_g[layer].reshape(1, d_c))


def _hgrn_sample_kernel(f_ref, i_ref, q_ref, g_ref, lb_ref, gn_ref, s0_ref, y_ref, s_ref, *, layer, ts):
    lb = _hgrn_lower_bound(lb_ref[...], layer)
    y, st_new = _hgrn_chunk(f_ref[...], i_ref[...], q_ref[...], g_ref[...], lb, gn_ref[...],
                            s0_ref[...].T, sub=f_ref.shape[0], n_valid=ts)
    y_ref[...] = y.astype(y_ref.dtype)
    s_ref[...] = st_new.T


def hgrn_sample(fiqg, state, lb_raw, norm_g, layer, ts, name):
    _, Bs, tp, d_c = fiqg.shape

    def col(k):
        return pl.BlockSpec((None, None, tp, DK_C), lambda b, h: (k, b, 0, h))

    st_spec = pl.BlockSpec((None, None, DK_C, DK_C), lambda b, h: (b, h, 0, 0))
    return pl.pallas_call(
        functools.partial(_hgrn_sample_kernel, layer=layer, ts=ts),
        grid=(Bs, H_C),
        in_specs=[col(0), col(1), col(2), col(3),
                  pl.BlockSpec((lb_raw.shape[0], DK_C), lambda b, h: (0, h)),
                  pl.BlockSpec((1, DK_C), lambda b, h: (0, h)),
                  st_spec],
        out_specs=[pl.BlockSpec((None, tp, DK_C), lambda b, h: (b, 0, h)), st_spec],
        out_shape=[jax.ShapeDtypeStruct((Bs, tp, d_c), BF16),
                   jax.ShapeDtypeStruct(state.shape, F32)],
        compiler_params=_cparams("parallel", "parallel"),
        name=name,
    )(fiqg, fiqg, fiqg, fiqg, lb_raw, norm_g[layer].reshape(1, d_c), state)


def _ffn_act_kernel(g_ref, u_ref, h_ref, w_ref, o_ref, gbuf, *, stride, halo, tiles_per_seq, halo_is_state):
    tt = g_ref.shape[0]
    if halo_is_state:
        gbuf[0:halo, :] = h_ref[...]
    else:
        first = (pl.program_id(1) % tiles_per_seq) == 0
        gbuf[0:halo, :] = jnp.where(first, 0.0, h_ref[...])
    gbuf[halo:halo + tt, :] = g_ref[...]
    a = w_ref[FFN_CONV_W - 1:FFN_CONV_W, :] * g_ref[...]
    for j in range(FFN_CONV_W - 1):
        s = halo - (FFN_CONV_W - 1 - j) * stride
        a = a + w_ref[j:j + 1, :] * gbuf[s:s + tt, :]
    o_ref[...] = (_silu(a) * u_ref[...]).astype(o_ref.dtype)


def ffn_act(G, U, state, conv_w, layer, seq_len, stride, name):
    M, d_ff = G.shape
    tc = _pick(d_ff, (1408, 512, 256, 128))
    w = conv_w[layer]
    if state is None:
        tt = _pick(seq_len, (512, 256, 128))
        halo = SUBLANES
        tps = seq_len // tt
        hblk = tt // halo
        halo_spec = pl.BlockSpec((halo, tc), lambda c, i: (jnp.maximum(i * hblk - 1, 0), c))
        halo_arg = G
    else:
        tt = M
        halo = state.shape[0]
        tps = 1
        halo_spec = pl.BlockSpec((halo, tc), lambda c, i: (0, c))
        halo_arg = state
    return pl.pallas_call(
        functools.partial(_ffn_act_kernel, stride=stride, halo=halo, tiles_per_seq=tps,
                          halo_is_state=state is not None),
        grid=(d_ff // tc, M // tt),
        in_specs=[pl.BlockSpec((tt, tc), lambda c, i: (i, c)),
                  pl.BlockSpec((tt, tc), lambda c, i: (i, c)),
                  halo_spec,
                  pl.BlockSpec((FFN_CONV_W, tc), lambda c, i: (0, c))],
        out_specs=pl.BlockSpec((tt, tc), lambda c, i: (i, c)),
        out_shape=jax.ShapeDtypeStruct((M, d_ff), BF16),
        scratch_shapes=[pltpu.VMEM((halo + tt, tc), F32)],
        compiler_params=_cparams("parallel", "parallel"),
        name=name,
    )(G, U, halo_arg, w)


def kernel(x_prompt, x_sample, cache_k, cache_v, state_conv, state_hgrn, state_ffn_conv, page_table, hgrn_lower_bound, sb_bias, norm_mix_g, w_in, conv_w, conv_b, conv_ln_g, conv_ln_b, hgrn_norm_g, w_out_a, w_out_b, w_out_c, w_o, norm_ffn_g, w_gate, w_up, ffn_conv_w, w_down, norm_final_g):
    Bp, T, D = x_prompt.shape
    Bs, Ts, _ = x_sample.shape
    depth = w_in.shape[0]
    d_a = H_A * DH_A
    d_b = conv_w.shape[2]
    d_c = H_C * DK_C
    d_ff = w_gate.shape[2]
    off_glu = 3 * d_a
    off_f = off_glu + 2 * d_b
    off_gates = off_f + 4 * d_c
    tp = 16

    n_phys, page = cache_k.shape[1], cache_k.shape[2]
    ckt = cache_k.transpose(0, 1, 3, 4, 2).reshape(depth, n_phys, d_a, page)
    cvt = cache_v.transpose(0, 1, 3, 4, 2).reshape(depth, n_phys, d_a, page)
    xp = x_prompt.reshape(Bp * T, D)
    xs = x_sample.transpose(1, 0, 2).reshape(Ts * Bs, D)

    def to_batch_major(a):
        return a.reshape(Ts, Bs, a.shape[-1]).transpose(1, 0, 2)

    def to_time_major(a):
        return a.transpose(1, 0, 2).reshape(Ts * Bs, a.shape[-1])

    outs = {k: [] for k in ("kp", "vp", "ks", "vs", "cp", "cs", "hp", "hs", "fp", "fs")}
    for l in range(depth):
        tag = f"l{l}"
        (pp,) = norm_matmul(xp, norm_mix_g, [w_in], l, f"in_p_{tag}")
        oa = attn_prompt(pp, sb_bias, l, Bp, T, f"attn_p_{tag}")
        ob, tail = conv_module(pp, off_glu, d_b, None, conv_w, conv_b, conv_ln_g, conv_ln_b, l, T, 1,
                               f"conv_p_{tag}")
        oc, hst = hgrn_prompt(pp, off_f, hgrn_lower_bound, hgrn_norm_g, l, Bp, T, f"hgrn_p_{tag}")
        mg = merge_branches(oa, ob, oc, pp, off_gates, w_out_a, w_out_b, w_out_c, l, f"merge_p_{tag}")
        xp = matmul_residual(xp, mg, w_o, l, f"wo_p_{tag}")
        gp, up = norm_matmul(xp, norm_ffn_g, [w_gate, w_up], l, f"ffn_in_p_{tag}")
        act = ffn_act(gp, up, None, ffn_conv_w, l, T, 1, f"ffn_act_p_{tag}")
        xp = matmul_residual(xp, act, w_down, l, f"down_p_{tag}")
        outs["kp"].append(pp[:, d_a:2 * d_a].reshape(Bp, T, H_A, DH_A))
        outs["vp"].append(pp[:, 2 * d_a:3 * d_a].reshape(Bp, T, H_A, DH_A))
        outs["cp"].append(tail[:, tail.shape[1] - (CONV_W - 1):, :])
        outs["hp"].append(hst)
        outs["fp"].append(gp.reshape(Bp, T, d_ff)[:, T - (FFN_CONV_W - 1):, :])

        (ps,) = norm_matmul(xs, norm_mix_g, [w_in], l, f"in_s_{tag}")
        qkv = to_batch_major(ps[:, :3 * d_a])
        q_s, k_s, v_s = qkv[..., :d_a], qkv[..., d_a:2 * d_a], qkv[..., 2 * d_a:]
        new_t = lambda a: jnp.pad(a.transpose(0, 2, 1), ((0, 0), (0, 0), (0, page - Ts)))
        oa = attn_sample(q_s, new_t(k_s), new_t(v_s), ckt, cvt, page_table, sb_bias, l, f"attn_s_{tag}")
        oa = to_time_major(oa).astype(BF16)
        cst =state_conv[l].transpose(1, 0, 2).reshape((CONV_W - 1) * Bs, d_b)
        ob, tail = conv_module(ps, off_glu, d_b, cst, conv_w, conv_b, conv_ln_g, conv_ln_b, l, Ts, Bs,
                               f"conv_s_{tag}")
        fiqg = to_batch_major(ps[:, off_f:off_f + 4 * d_c]).reshape(Bs, Ts, 4, d_c)
        fiqg = jnp.pad(fiqg.transpose(2, 0, 1, 3), ((0, 0), (0, 0), (0, tp - Ts), (0, 0)))
        oc, hst = hgrn_sample(fiqg, state_hgrn[l], hgrn_lower_bound, hgrn_norm_g, l, Ts, f"hgrn_s_{tag}")
        oc = to_time_major(oc[:, :Ts, :])
        mg = merge_branches(oa, ob, oc, ps, off_gates, w_out_a, w_out_b, w_out_c, l, f"merge_s_{tag}")
        xs = matmul_residual(xs, mg, w_o, l, f"wo_s_{tag}")
        gs, us = norm_matmul(xs, norm_ffn_g, [w_gate, w_up], l, f"ffn_in_s_{tag}")
        fst = state_ffn_conv[l].transpose(1, 0, 2).reshape((FFN_CONV_W - 1) * Bs, d_ff)
        act = ffn_act(gs, us, fst, ffn_conv_w, l, Ts, Bs, f"ffn_act_s_{tag}")
        xs = matmul_residual(xs, act, w_down, l, f"down_s_{tag}")
        outs["ks"].append(k_s.reshape(Bs, Ts, H_A, DH_A))
        outs["vs"].append(v_s.reshape(Bs, Ts, H_A, DH_A))
        outs["cs"].append(tail.reshape(CONV_W - 1, Bs, d_b).transpose(1, 0, 2))
        outs["hs"].append(hst)
        ext_g = jnp.concatenate([fst, gs], axis=0)
        outs["fs"].append(ext_g[ext_g.shape[0] - (FFN_CONV_W - 1) * Bs:]
                          .reshape(FFN_CONV_W - 1, Bs, d_ff).transpose(1, 0, 2))

    y_p = rmsnorm_rows(xp, norm_final_g, "final_p").reshape(Bp, T, D)
    y_s = to_batch_major(rmsnorm_rows(xs, norm_final_g, "final_s"))
    st = lambda k: jnp.stack(outs[k])
    return (y_p, y_s, st("kp"), st("vp"), st("ks"), st("vs"), st("cp"), st("cs"),
            st("hp"), st("hs"), st("fp"), st("fs"))
```

```python
import functools

import jax
import jax.numpy as jnp
from jax import lax
from jax.experimental import pallas as pl
from jax.experimental.pallas import tpu as pltpu

F32 = jnp.float32
BF16 = jnp.bfloat16
EPS = 1e-6

VMEM_LIMIT_BYTES = 56 * 1024 * 1024
SUBLANES = 8
LANES = 128

H_A = 16
DH_A = 64
H_C = 8
DK_C = 128
CONV_W = 31
FFN_CONV_W = 3
HGRN_CHUNK = 64
HGRN_SUB = 8
HGRN_HEADS_PER_STEP = 4
ATT_TQ = 256
ATT_TK = 256
ATT_HEADS_PER_STEP = 4


def _cparams(*sem):
    return pltpu.CompilerParams(dimension_semantics=sem, vmem_limit_bytes=VMEM_LIMIT_BYTES)


def _pick(n, prefs):
    for p in prefs:
        if n % p == 0:
            return p
    return n


def _log_sigmoid_parts(z):
    l = jnp.log(1.0 + jnp.exp(_neg_abs(z)))
    return jnp.minimum(z, 0.0) - l, -(jnp.maximum(z, 0.0) + l)


def _silu(x):
    return x * jax.nn.sigmoid(x)


def _norm_mm_kernel(x_ref, g_ref, *refs, n_w):
    w_refs = refs[:n_w]
    o_refs = refs[n_w:2 * n_w]
    h_ref = refs[2 * n_w]

    @pl.when(pl.program_id(1) == 0)
    def _():
        xf = x_ref[...]
        r = lax.rsqrt(jnp.mean(xf * xf, axis=-1, keepdims=True) + EPS)
        h_ref[...] = ((xf * r) * g_ref[...]).astype(BF16)

    h = h_ref[...]
    for w_ref, o_ref in zip(w_refs, o_refs):
        o_ref[...] = jnp.dot(h, w_ref[...].astype(BF16), preferred_element_type=F32)


def norm_matmul(x, g, ws, layer, name):
    M, D = x.shape
    N = ws[0].shape[2]
    tm = _pick(M, (1024, 512, 256, 128))
    tn = _pick(N, (512, 256, 128))
    n_w = len(ws)
    g3 = g.reshape(g.shape[0], 1, D)
    outs = pl.pallas_call(
        functools.partial(_norm_mm_kernel, n_w=n_w),
        grid=(M // tm, N // tn),
        in_specs=[pl.BlockSpec((tm, D), lambda i, j: (i, 0)),
                  pl.BlockSpec((None, 1, D), lambda i, j: (layer, 0, 0))]
                 + [pl.BlockSpec((None, D, tn), lambda i, j: (layer, 0, j)) for _ in ws],
        out_specs=[pl.BlockSpec((tm, tn), lambda i, j: (i, j)) for _ in ws],
        out_shape=[jax.ShapeDtypeStruct((M, N), F32) for _ in ws],
        scratch_shapes=[pltpu.VMEM((tm, D), BF16)],
        compiler_params=_cparams("parallel", "arbitrary"),
        name=name,
    )(x, g3, *ws)
    return outs


def _rmsnorm_kernel(x_ref, g_ref, o_ref):
    xf = x_ref[...]
    r = lax.rsqrt(jnp.mean(xf * xf, axis=-1, keepdims=True) + EPS)
    o_ref[...] = (xf * r) * g_ref[...]


def rmsnorm_rows(x, g, name):
    M, D = x.shape
    tm = _pick(M, (512, 256, 128))
    return pl.pallas_call(
        _rmsnorm_kernel,
        grid=(M // tm,),
        in_specs=[pl.BlockSpec((tm, D), lambda i: (i, 0)),
                  pl.BlockSpec((1, D), lambda i: (0, 0))],
        out_specs=pl.BlockSpec((tm, D), lambda i: (i, 0)),
        out_shape=jax.ShapeDtypeStruct((M, D), F32),
        compiler_params=_cparams("parallel"),
        name=name,
    )(x, g.reshape(1, D))


def _mm_res_kernel(x_ref, a_ref, w_ref, o_ref):
    o_ref[...] = x_ref[...] + jnp.dot(a_ref[...], w_ref[...].astype(BF16),
                                      preferred_element_type=F32)


def matmul_residual(x, a, w, layer, name):
    M, N = x.shape
    K = a.shape[1]
    tm = _pick(M, (1024, 512, 256, 128))
    tn = 256 if K > 4096 else _pick(N, (512, 256, 128))
    return pl.pallas_call(
        _mm_res_kernel,
        grid=(M // tm, N // tn),
        in_specs=[pl.BlockSpec((tm, tn), lambda i, j: (i, j)),
                  pl.BlockSpec((tm, K), lambda i, j: (i, 0)),
                  pl.BlockSpec((None, K, tn), lambda i, j: (layer, 0, j))],
        out_specs=pl.BlockSpec((tm, tn), lambda i, j: (i, j)),
        out_shape=jax.ShapeDtypeStruct((M, N), F32),
        compiler_params=_cparams("parallel", "arbitrary"),
        name=name,
    )(x, a, w)


def _merge_kernel(oa_ref, ob_ref, oc_ref, g0_ref, g1_ref, g2_ref, wa_ref, wb_ref, wc_ref, o_ref):
    def branch(o_r, g_r, w_r):
        y = jnp.dot(o_r[...], w_r[...].astype(BF16), preferred_element_type=F32)
        return jax.nn.sigmoid(g_r[...]) * y

    m = branch(oa_ref, g0_ref, wa_ref) + branch(ob_ref, g1_ref, wb_ref) + branch(oc_ref, g2_ref, wc_ref)
    o_ref[...] = m.astype(BF16)


def merge_branches(oa, ob, oc, proj, gate_off, wa, wb, wc, layer, name):
    M, K = oa.shape
    N = wa.shape[2]
    tm = _pick(M, (1024, 512, 256, 128))
    tn = _pick(N, (512, 256, 128))
    gb = gate_off // tn
    nb = N // tn
    o_spec = pl.BlockSpec((tm, K), lambda i, j: (i, 0))
    w_spec = pl.BlockSpec((None, K, tn), lambda i, j: (layer, 0, j))
    return pl.pallas_call(
        _merge_kernel,
        grid=(M // tm, nb),
        in_specs=[o_spec, o_spec, o_spec,
                  pl.BlockSpec((tm, tn), lambda i, j: (i, gb + j)),
                  pl.BlockSpec((tm, tn), lambda i, j: (i, gb + nb + j)),
                  pl.BlockSpec((tm, tn), lambda i, j: (i, gb + 2 * nb + j)),
                  w_spec, w_spec, w_spec],
        out_specs=pl.BlockSpec((tm, tn), lambda i, j: (i, j)),
        out_shape=jax.ShapeDtypeStruct((M, N), BF16),
        compiler_params=_cparams("parallel", "arbitrary"),
        name=name,
    )(oa, ob, oc, proj, proj, proj, wa, wb, wc)


LOG2E = 1.4426950408889634


def _strict_upper2(n):
    r = lax.broadcasted_iota(jnp.int32, (2 * n, n), 0)
    c = lax.broadcasted_iota(jnp.int32, (2 * n, n), 1)
    r = jnp.where(r >= n, r - n, r)
    return jnp.where(r > c, 1.0, 0.0).astype(BF16)


def _neg_abs(x):
    bits = pltpu.bitcast(x, jnp.uint32) | jnp.uint32(0x80000000)
    return pltpu.bitcast(bits, F32)


MASKED_LOGIT = -1e30


def _sb_tiles(z2s, v_bfs, u2, crys, chained=False, v_transposed=False):
    staged = []
    for z2 in z2s:
        l2 = jnp.log2(1.0 + jnp.exp2(_neg_abs(z2)))
        log_beta = jnp.minimum(z2, 0.0) - l2
        neg_log_rem = jnp.maximum(z2, 0.0) + l2
        hi = neg_log_rem.astype(BF16)
        if u2.shape[0] == 2 * u2.shape[1]:
            lo = (neg_log_rem - hi.astype(F32)).astype(BF16)
            hi = jnp.concatenate([hi, lo], axis=1)
        suffix = jnp.dot(hi, u2, preferred_element_type=F32)
        staged.append((log_beta, suffix, jnp.sum(neg_log_rem, axis=-1, keepdims=True)))
    contribs, new_crys = [], []
    for n, ((log_beta, suffix, row_sum), v_bf) in enumerate(zip(staged, v_bfs)):
        cry = crys if chained else crys[n]
        w = jnp.exp2(log_beta - suffix - cry).astype(BF16)
        v_contract = 1 if v_transposed else 0
        contribs.append(lax.dot_general(w, v_bf, (((1,), (v_contract,)), ((), ())),
                                        preferred_element_type=F32))
        if chained:
            crys = cry + row_sum
        else:
            new_crys.append(cry + row_sum)
    return contribs, (crys if chained else new_crys)


def _attn_prompt_kernel(bias_ref, q_ref, k_ref, v_ref, o_ref, kb_ref, vb_ref, acc_ref, cry_ref, *,
                        layer, heads_per_blk):
    hp = pl.program_id(1)
    i = pl.program_id(2)
    tq = q_ref.shape[0]
    tk = ATT_TK
    dh = DH_A

    heads = range(heads_per_blk)

    @pl.when(i == 0)
    def _():
        lane = lax.broadcasted_iota(jnp.int32, (k_ref.shape[0], dh), 1)
        for hh in heads:
            b2 = jnp.full(lane.shape, bias_ref[layer * H_A + hp * heads_per_blk + hh] * LOG2E, F32)
            b_hi = b2.astype(BF16).astype(F32)
            extra = jnp.where(lane == 0, b_hi, jnp.where(lane == 1, b2 - b_hi, 0.0)).astype(BF16)
            kb_ref[hh] = jnp.concatenate([k_ref[:, hh * dh:(hh + 1) * dh].astype(BF16), extra], axis=1)
            vb_ref[hh] = v_ref[:, hh * dh:(hh + 1) * dh].astype(BF16)

    u2 = _strict_upper2(tk)[:tk]
    diag_mask = (lax.broadcasted_iota(jnp.int32, (tq, tk), 1)
                 < lax.broadcasted_iota(jnp.int32, (tq, tk), 0))
    ones2 = jnp.where(lax.broadcasted_iota(jnp.int32, (tq, dh), 1) < 2, 1.0, 0.0).astype(BF16)
    q2 = [jnp.concatenate([(q_ref[:, hh * dh:(hh + 1) * dh] * (dh ** -0.5 * LOG2E)).astype(BF16), ones2],
                          axis=1) for hh in heads]

    def tile(j, mask, first):
        rows = pl.ds(pl.multiple_of(j * tk, tk), tk)
        z2s = [lax.dot_general(q2[hh], kb_ref[hh, rows, :], (((1,), (1,)), ((), ())),
                               preferred_element_type=F32) for hh in heads]
        if mask is not None:
            z2s = [jnp.where(mask, z2, MASKED_LOGIT) for z2 in z2s]
        crys = [jnp.zeros((tq, 1), F32) if first else cry_ref[hh] for hh in heads]
        contribs, crys = _sb_tiles(z2s, [vb_ref[hh, rows, :] for hh in heads], u2, crys)
        for hh in heads:
            acc_ref[hh] = contribs[hh] if first else acc_ref[hh] + contribs[hh]
            cry_ref[hh] = crys[hh]

    tile(i, diag_mask, True)

    def body(jj, carry):
        tile(i - 1 - jj, None, False)
        return carry

    lax.fori_loop(0, i, body, 0)
    for hh in heads:
        o_ref[:, hh * dh:(hh + 1) * dh] = acc_ref[hh].astype(o_ref.dtype)


def attn_prompt(proj, sb_bias, layer, B, T, name):
    d_a = H_A * DH_A
    hpb = ATT_HEADS_PER_STEP
    wb = hpb * DH_A
    n_hp = H_A // hpb
    nq = T // ATT_TQ
    kv_blk = d_a // wb
    return pl.pallas_call(
        functools.partial(_attn_prompt_kernel, layer=layer, heads_per_blk=hpb),
        grid=(B, n_hp, nq),
        in_specs=[pl.BlockSpec(memory_space=pltpu.SMEM),
                  pl.BlockSpec((ATT_TQ, wb), lambda b, h, i: (b * nq + i, h)),
                  pl.BlockSpec((T, wb), lambda b, h, i: (b, kv_blk + h)),
                  pl.BlockSpec((T, wb), lambda b, h, i: (b, 2 * kv_blk + h))],
        out_specs=pl.BlockSpec((ATT_TQ, wb), lambda b, h, i: (b * nq + i, h)),
        out_shape=jax.ShapeDtypeStruct((B * T, d_a), BF16),
        scratch_shapes=[pltpu.VMEM((hpb, T, 2 * DH_A), BF16), pltpu.VMEM((hpb, T, DH_A), BF16),
                        pltpu.VMEM((hpb, ATT_TQ, DH_A), F32), pltpu.VMEM((hpb, ATT_TQ, 1), F32)],
        compiler_params=_cparams("parallel", "parallel", "arbitrary"),
        name=name,
    )(sb_bias.reshape(-1), proj, proj, proj)


def _attn_sample_kernel(pt_ref, q_ref, kn_ref, vn_ref, bias_ref, *refs, n_pg):
    del pt_ref
    k_refs = refs[:n_pg]
    v_refs = refs[n_pg:2 * n_pg]
    o_ref = refs[2 * n_pg]
    qbd_ref, acc_ref, cry_ref = refs[2 * n_pg + 1:]
    g = pl.program_id(1)
    ts, d_a = q_ref.shape
    rows = ts * H_A
    page = k_refs[0].shape[1]
    head_mask = (lax.broadcasted_iota(jnp.int32, (H_A, d_a), 1) // DH_A
                 == lax.broadcasted_iota(jnp.int32, (H_A, d_a), 0))
    bias2 = bias_ref[...] * LOG2E
    u2 = _strict_upper2(page)

    def logits(kt_ref):
        return jnp.dot(qbd_ref[...], kt_ref[...].astype(BF16), preferred_element_type=F32) + bias2

    @pl.when(g == 0)
    def _():
        q = q_ref[...] * (DH_A ** -0.5 * LOG2E)
        qbd_ref[...] = jnp.concatenate(
            [jnp.where(head_mask, jnp.broadcast_to(q[t:t + 1, :], (H_A, d_a)), 0.0) for t in range(ts)],
            axis=0).astype(BF16)
        t_of_row = lax.broadcasted_iota(jnp.int32, (rows, page), 0) // H_A
        mask = lax.broadcasted_iota(jnp.int32, (rows, page), 1) < t_of_row
        contribs, crys = _sb_tiles([jnp.where(mask, logits(kn_ref), MASKED_LOGIT)],
                                   [vn_ref[...].astype(BF16)], u2, [jnp.zeros((rows, 1), F32)],
                                   v_transposed=True)
        acc_ref[...] = contribs[0]
        cry_ref[...] = crys[0]

    contribs, cry = _sb_tiles([logits(k_refs[r]) for r in range(n_pg)],
                              [v_refs[r][...].astype(BF16) for r in range(n_pg)],
                              u2, cry_ref[...], chained=True, v_transposed=True)
    acc = acc_ref[...]
    for contrib in contribs:
        acc = acc + contrib
    acc_ref[...] = acc
    cry_ref[...] = cry

    @pl.when(g == pl.num_programs(1) - 1)
    def _():
        hm = head_mask.astype(F32)
        for t in range(ts):
            o_ref[t:t + 1, :] = jnp.sum(acc_ref[t * H_A:(t + 1) * H_A, :] * hm, axis=0, keepdims=True)


def attn_sample(q, k_new_t, v_new_t, cache_kt, cache_vt, page_table, sb_bias, layer, name):
    Bs, Ts, d_a = q.shape
    n_pages = page_table.shape[1]
    page = cache_kt.shape[3]
    n_pg = _pick(n_pages, (8, 4, 2, 1))
    n_grp = n_pages // n_pg
    bias_col = jnp.tile(sb_bias[layer], Ts).reshape(Ts * H_A, 1)

    def page_spec(r):
        return pl.BlockSpec((None, None, d_a, page),
                            lambda b, g, pt: (layer, pt[b, n_pages - 1 - (g * n_pg + r)], 0, 0))

    new_spec = pl.BlockSpec((None, d_a, page), lambda b, g, pt: (b, 0, 0))
    grid_spec = pltpu.PrefetchScalarGridSpec(
        num_scalar_prefetch=1,
        grid=(Bs, n_grp),
        in_specs=[pl.BlockSpec((None, Ts, d_a), lambda b, g, pt: (b, 0, 0)), new_spec, new_spec,
                  pl.BlockSpec((Ts * H_A, 1), lambda b, g, pt: (0, 0))]
                 + [page_spec(r) for r in range(n_pg)] + [page_spec(r) for r in range(n_pg)],
        out_specs=pl.BlockSpec((None, Ts, d_a), lambda b, g, pt: (b, 0, 0)),
        scratch_shapes=[pltpu.VMEM((Ts * H_A, d_a), BF16),
                        pltpu.VMEM((Ts * H_A, d_a), F32),
                        pltpu.VMEM((Ts * H_A, 1), F32)],
    )
    return pl.pallas_call(
        functools.partial(_attn_sample_kernel, n_pg=n_pg),
        grid_spec=grid_spec,
        out_shape=jax.ShapeDtypeStruct((Bs, Ts, d_a), F32),
        compiler_params=_cparams("parallel", "arbitrary"),
        name=name,
    )(page_table, q, k_new_t, v_new_t, bias_col, *([cache_kt] * n_pg), *([cache_vt] * n_pg))


def _conv_kernel(a_ref, b_ref, ha_ref, hb_ref, w_ref, cb_ref, lg_ref, lb_ref, y_ref, tail_ref,
                 ubuf, cbuf, *, stride, halo, tiles_per_seq, halo_is_state):
    tt = a_ref.shape[0]
    d_b = a_ref.shape[1]
    if halo_is_state:
        ubuf[0:halo, :] = ha_ref[...]
    else:
        first = (pl.program_id(0) % tiles_per_seq) == 0
        uh = ha_ref[...] * jax.nn.sigmoid(hb_ref[...])
        ubuf[0:halo, :] = jnp.where(first, 0.0, uh)
    ubuf[halo:halo + tt, :] = a_ref[...] * jax.nn.sigmoid(b_ref[...])

    base = halo - (CONV_W - 1) * stride
    for c in range(d_b // LANES):
        cols = slice(c * LANES, (c + 1) * LANES)
        acc = jnp.zeros((tt, LANES), F32)
        for j in range(CONV_W):
            s = base + j * stride
            acc = acc + w_ref[j:j + 1, cols] * ubuf[s:s + tt, cols]
        cbuf[:, cols] = acc

    cv = cbuf[...] + cb_ref[...]
    mu = jnp.mean(cv, axis=-1, keepdims=True)
    d = cv - mu
    var = jnp.mean(d * d, axis=-1, keepdims=True)
    y = (d * lax.rsqrt(var + EPS)) * lg_ref[...] + lb_ref[...]
    y_ref[...] = _silu(y).astype(y_ref.dtype)
    n_tail = tail_ref.shape[0]
    tail_ref[...] = ubuf[halo + tt - n_tail:halo + tt, :]


def conv_module(proj, glu_off, d_b, state, conv_w, conv_b, ln_g, ln_b, layer, seq_len, stride, name):
    M = proj.shape[0]
    ca = glu_off // d_b
    w = conv_w[layer]
    vecs = [conv_b[layer].reshape(1, d_b), ln_g[layer].reshape(1, d_b), ln_b[layer].reshape(1, d_b)]
    vec_spec = pl.BlockSpec((1, d_b), lambda i: (0, 0))
    if state is None:
        tt = _pick(seq_len, (256, 128))
        halo = 32
        tps = seq_len // tt
        n_seq = M // seq_len
        hblk = tt // halo
        halo_specs = [pl.BlockSpec((halo, d_b), lambda i: (jnp.maximum(i * hblk - 1, 0), ca)),
                      pl.BlockSpec((halo, d_b), lambda i: (jnp.maximum(i * hblk - 1, 0), ca + 1))]
        halo_args = [proj, proj]
        n_tail = halo
        tail_shape = (n_seq, n_tail, d_b)
        tail_spec = pl.BlockSpec((None, n_tail, d_b), lambda i: (i // tps, 0, 0))
        grid = (M // tt,)
    else:
        tt = M
        halo = state.shape[0]
        tps = 1
        halo_specs = [pl.BlockSpec((halo, d_b), lambda i: (0, 0)),
                      pl.BlockSpec((halo, d_b), lambda i: (0, 0))]
        halo_args = [state, state]
        n_tail = halo
        tail_shape = (n_tail, d_b)
        tail_spec = pl.BlockSpec((n_tail, d_b), lambda i: (0, 0))
        grid = (1,)
    return pl.pallas_call(
        functools.partial(_conv_kernel, stride=stride, halo=halo, tiles_per_seq=tps,
                          halo_is_state=state is not None),
        grid=grid,
        in_specs=[pl.BlockSpec((tt, d_b), lambda i: (i, ca)),
                  pl.BlockSpec((tt, d_b), lambda i: (i, ca + 1))] + halo_specs
                 + [pl.BlockSpec((CONV_W, d_b), lambda i: (0, 0)), vec_spec, vec_spec, vec_spec],
        out_specs=[pl.BlockSpec((tt, d_b), lambda i: (i, 0)), tail_spec],
        out_shape=[jax.ShapeDtypeStruct((M, d_b), BF16), jax.ShapeDtypeStruct(tail_shape, F32)],
        scratch_shapes=[pltpu.VMEM((halo + tt, d_b), F32), pltpu.VMEM((tt, d_b), F32)],
        compiler_params=_cparams("arbitrary"),
        name=name,
    )(proj, proj, *halo_args, w, *vecs)


def _hgrn_lower_bound(lbraw, layer):
    e = jnp.exp(lbraw - jnp.max(lbraw, axis=0, keepdims=True))
    p = e / jnp.sum(e, axis=0, keepdims=True)
    lb = jnp.zeros((1, lbraw.shape[1]), F32)
    for r in range(1, layer + 1):
        lb = lb + p[r:r + 1, :]
    return lb


def _cumsum_rows(x):
    n = x.shape[0]
    r = lax.broadcasted_iota(jnp.int32, (n, n), 0)
    c = lax.broadcasted_iota(jnp.int32, (n, n), 1)
    tri = jnp.where(r >= c, 1.0, 0.0).astype(BF16)
    hi = x.astype(BF16)
    r1 = x - hi.astype(F32)
    mid = r1.astype(BF16)
    lo = (r1 - mid.astype(F32)).astype(BF16)
    parts = jnp.dot(tri, jnp.concatenate([hi, mid, lo], axis=1), preferred_element_type=F32)
    w = x.shape[1]
    return parts[:, :w] + parts[:, w:2 * w] + parts[:, 2 * w:]


def _hgrn_chunks(heads, *, sub, n_valid):
    c = heads[0][0].shape[0]
    nt = (((1,), (1,)), ((), ()))
    n_sub = -(-min(c, n_valid) // sub)
    row_c = lax.broadcasted_iota(jnp.int32, (c, 1), 0)
    row_l = lax.broadcasted_iota(jnp.int32, (sub, 1), 0)

    kg = []
    for zf, iv, q, gate, lb, gn, st in heads:
        log_sig, log_sig_neg = _log_sigmoid_parts(zf)
        a_ = jnp.log(lb)
        b_ = jnp.log1p(-lb) + log_sig
        log_f = jnp.maximum(a_, b_) + jnp.log(1.0 + jnp.exp(_neg_abs(a_ - b_)))
        k = (1.0 - lb) * jnp.exp(log_sig_neg)
        if n_valid < c:
            valid = lax.broadcasted_iota(jnp.int32, zf.shape, 0) < n_valid
            log_f = jnp.where(valid, log_f, 0.0)
            k = jnp.where(valid, k, 0.0)
        kg.append((k, _cumsum_rows(log_f)))

    o_inter = [lax.dot_general((q * jnp.exp(G)).astype(BF16), st.astype(BF16), nt,
                               preferred_element_type=F32)
               for (zf, iv, q, gate, lb, gn, st), (k, G) in zip(heads, kg)]

    scores = []
    for (zf, iv, q, gate, lb, gn, st), (k, G) in zip(heads, kg):
        per = []
        for I in range(1, n_sub):
            lo = I * sub
            ref = G[lo - 1:lo]
            qd = q[lo:lo + sub] * jnp.exp(G[lo:lo + sub] - ref)
            kd = jnp.where(row_c < lo, k * jnp.exp(jnp.minimum(ref - G, 0.0)), 0.0)
            per.append(lax.dot_general(qd.astype(BF16), kd.astype(BF16), nt, preferred_element_type=F32))
        scores.append(per)

    diag = []
    for (zf, iv, q, gate, lb, gn, st), (k, G) in zip(heads, kg):
        per = []
        for I in range(n_sub):
            lo = I * sub
            qI, GI = q[lo:lo + sub], G[lo:lo + sub]
            o = jnp.zeros((sub, iv.shape[1]), F32)
            for s in range(min(sub, n_valid - lo)):
                sg = lo + s
                e = jnp.exp(jnp.minimum(GI - G[sg:sg + 1], 0.0))
                col = jnp.sum(qI * (k[sg:sg + 1] * e), axis=-1, keepdims=True)
                o = o + jnp.where(row_l >= s, col, 0.0) * iv[sg:sg + 1]
            per.append(o)
        diag.append(per)

    results = []
    for n, ((zf, iv, q, gate, lb, gn, st), (k, G)) in enumerate(zip(heads, kg)):
        v_bf = iv.astype(BF16)
        outs = []
        for I in range(c // sub):
            lo = I * sub
            if I >= n_sub:
                outs.append(jnp.zeros((sub, iv.shape[1]), F32))
                continue
            o = o_inter[n][lo:lo + sub] + diag[n][I]
            if I > 0:
                o = o + jnp.dot(scores[n][I - 1].astype(BF16), v_bf, preferred_element_type=F32)
            outs.append(o)
        o = jnp.concatenate(outs, axis=0) if len(outs) > 1 else outs[0]
        g_end = G[c - 1:c]
        kdec = k * jnp.exp(g_end - G)
        st_new = st * jnp.exp(g_end) + lax.dot_general(v_bf, kdec.astype(BF16), (((0,), (0,)), ((), ())),
                                                      preferred_element_type=F32)
        r = lax.rsqrt(jnp.mean(o * o, axis=-1, keepdims=True) + EPS)
        results.append((((o * r) * gn) * _silu(gate), st_new))
    return results


def _hgrn_prompt_kernel(f_ref, i_ref, q_ref, g_ref, lb_ref, gn_ref, y_ref, s_ref, st_ref, *, layer, nh):
    tt = f_ref.shape[0]
    c = HGRN_CHUNK
    n = DK_C
    ti = pl.program_id(2)
    lbs = _hgrn_lower_bound(lb_ref[...], layer)
    gns = gn_ref[...]

    @pl.when(ti == 0)
    def _():
        st_ref[...] = jnp.zeros_like(st_ref)

    def body(ci, carry):
        rows = pl.ds(pl.multiple_of(ci * c, c), c)
        heads = []
        for h in range(nh):
            cols = slice(h * n, (h + 1) * n)
            heads.append((f_ref[rows, cols], i_ref[rows, cols], q_ref[rows, cols], g_ref[rows, cols],
                          lbs[:, cols], gns[:, cols], st_ref[h]))
        for h, (y, st_new) in enumerate(_hgrn_chunks(heads, sub=HGRN_SUB, n_valid=c)):
            y_ref[rows, h * n:(h + 1) * n] = y.astype(y_ref.dtype)
            st_ref[h] = st_new
        return carry

    lax.fori_loop(0, tt // c, body, 0)

    @pl.when(ti == pl.num_programs(2) - 1)
    def _():
        for h in range(nh):
            s_ref[h] = st_ref[h].T


def hgrn_prompt(proj, f_off, lb_raw, norm_g, layer, B, T, name):
    d_c = H_C * DK_C
    nh = HGRN_HEADS_PER_STEP
    wb = nh * DK_C
    n_hg = H_C // nh
    tt = _pick(T, (1024, 512, 256, 128, 64))
    nt = T // tt
    fb = f_off // wb

    def col(k):
        return pl.BlockSpec((tt, wb), lambda b, h, t: (b * nt + t, fb + k * n_hg + h))

    return pl.pallas_call(
        functools.partial(_hgrn_prompt_kernel, layer=layer, nh=nh),
        grid=(B, n_hg, nt),
        in_specs=[col(0), col(1), col(2), col(3),
                  pl.BlockSpec((lb_raw.shape[0], wb), lambda b, h, t: (0, h)),
                  pl.BlockSpec((1, wb), lambda b, h, t: (0, h))],
        out_specs=[pl.BlockSpec((tt, wb), lambda b, h, t: (b * nt + t, h)),
                   pl.BlockSpec((None, nh, DK_C, DK_C), lambda b, h, t: (b, h, 0, 0))],
        out_shape=[jax.ShapeDtypeStruct((B * T, d_c), BF16),
                   jax.ShapeDtypeStruct((B, H_C, DK_C, DK_C), F32)],
        scratch_shapes=[pltpu.VMEM((nh, DK_C, DK_C), F32)],
        compiler_params=_cparams("parallel", "parallel", "arbitrary"),
        name=name,
    )(proj, proj, proj, proj, lb_raw, norm_g[layer].reshape(1, d_c))


def _hgrn_sample_kernel(f_ref, i_ref, q_ref, g_ref, lb_ref, gn_ref, s0_ref, y_ref, s_ref, *, layer, ts, nh):
    n = DK_C
    lbs = _hgrn_lower_bound(lb_ref[...], layer)
    gns = gn_ref[...]
    heads = []
    for h in range(nh):
        cols = slice(h * n, (h + 1) * n)
        heads.append((f_ref[:, cols], i_ref[:, cols], q_ref[:, cols], g_ref[:, cols],
                      lbs[:, cols], gns[:, cols], s0_ref[h].T))
    for h, (y, st_new) in enumerate(_hgrn_chunks(heads, sub=f_ref.shape[0], n_valid=ts)):
        y_ref[:, h * n:(h + 1) * n] = y.astype(y_ref.dtype)
        s_ref[h] = st_new.T


def hgrn_sample(fiqg, state, lb_raw, norm_g, layer, ts, name):
    _, Bs, tp, d_c = fiqg.shape
    nh = H_C
    wb = nh * DK_C

    def col(k):
        return pl.BlockSpec((None, None, tp, wb), lambda b, h: (k, b, 0, h))

    st_spec = pl.BlockSpec((None, nh, DK_C, DK_C), lambda b, h: (b, h, 0, 0))
    return pl.pallas_call(
        functools.partial(_hgrn_sample_kernel, layer=layer, ts=ts, nh=nh),
        grid=(Bs, H_C // nh),
        in_specs=[col(0), col(1), col(2), col(3),
                  pl.BlockSpec((lb_raw.shape[0], wb), lambda b, h: (0, h)),
                  pl.BlockSpec((1, wb), lambda b, h: (0, h)),
                  st_spec],
        out_specs=[pl.BlockSpec((None, tp, wb), lambda b, h: (b, 0, h)), st_spec],
        out_shape=[jax.ShapeDtypeStruct((Bs, tp, d_c), BF16),
                   jax.ShapeDtypeStruct(state.shape, F32)],
        compiler_params=_cparams("parallel", "parallel"),
        name=name,
    )(fiqg, fiqg, fiqg, fiqg, lb_raw, norm_g[layer].reshape(1, d_c), state)


def _ffn_act_kernel(g_ref, u_ref, h_ref, w_ref, o_ref, gbuf, *, stride):
    tt = g_ref.shape[0]
    halo = h_ref.shape[0]
    gbuf[0:halo, :] = h_ref[...]
    gbuf[halo:halo + tt, :] = g_ref[...]
    a = w_ref[FFN_CONV_W - 1:FFN_CONV_W, :] * g_ref[...]
    for j in range(FFN_CONV_W - 1):
        s = halo - (FFN_CONV_W - 1 - j) * stride
        a = a + w_ref[j:j + 1, :] * gbuf[s:s + tt, :]
    o_ref[...] = (_silu(a) * u_ref[...]).astype(o_ref.dtype)


def _ffn_in_act_kernel(x_ref, g_ref, wg_ref, wu_ref, cw_ref, act_ref, tail_ref, h_ref, gbuf, carry_ref, *,
                       tiles_per_seq):
    i = pl.program_id(0)
    j = pl.program_id(1)
    tm = act_ref.shape[0]
    halo = tail_ref.shape[0]

    @pl.when(j == 0)
    def _():
        xf = x_ref[...]
        r = lax.rsqrt(jnp.mean(xf * xf, axis=-1, keepdims=True) + EPS)
        h_ref[...] = ((xf * r) * g_ref[...]).astype(BF16)

    h = h_ref[...]
    gate = jnp.dot(h, wg_ref[...].astype(BF16), preferred_element_type=F32)
    up = jnp.dot(h, wu_ref[...].astype(BF16), preferred_element_type=F32)
    first = (i % tiles_per_seq) == 0
    gbuf[0:halo, :] = jnp.where(first, 0.0, carry_ref[j])
    gbuf[halo:halo + tm, :] = gate
    a = cw_ref[FFN_CONV_W - 1:FFN_CONV_W, :] * gate
    for k in range(FFN_CONV_W - 1):
        s = halo - (FFN_CONV_W - 1 - k)
        a = a + cw_ref[k:k + 1, :] * gbuf[s:s + tm, :]
    act_ref[...] = (_silu(a) * up).astype(act_ref.dtype)
    last = gbuf[tm:tm + halo, :]
    carry_ref[j] = last
    tail_ref[...] = last


def ffn_in_act(x, g, w_gate, w_up, conv_w, layer, seq_len, name):
    M, D = x.shape
    N = w_gate.shape[2]
    tm = _pick(seq_len, (1024, 512, 256, 128))
    tn = _pick(N, (512, 256, 128))
    tps = seq_len // tm
    halo = SUBLANES
    g3 = g.reshape(g.shape[0], 1, D)
    w_spec = pl.BlockSpec((None, D, tn), lambda i, j: (layer, 0, j))
    return pl.pallas_call(
        functools.partial(_ffn_in_act_kernel, tiles_per_seq=tps),
        grid=(M // tm, N // tn),
        in_specs=[pl.BlockSpec((tm, D), lambda i, j: (i, 0)),
                  pl.BlockSpec((None, 1, D), lambda i, j: (layer, 0, 0)),
                  w_spec, w_spec,
                  pl.BlockSpec((None, FFN_CONV_W, tn), lambda i, j: (layer, 0, j))],
        out_specs=[pl.BlockSpec((tm, tn), lambda i, j: (i, j)),
                   pl.BlockSpec((None, halo, tn), lambda i, j: (i, 0, j))],
        out_shape=[jax.ShapeDtypeStruct((M, N), BF16),
                   jax.ShapeDtypeStruct((M // tm, halo, N), F32)],
        scratch_shapes=[pltpu.VMEM((tm, D), BF16), pltpu.VMEM((halo + tm, tn), F32),
                        pltpu.VMEM((N // tn, halo, tn), F32)],
        compiler_params=_cparams("arbitrary", "arbitrary"),
        name=name,
    )(x, g3, w_gate, w_up, conv_w)


def ffn_act(G, U, state, conv_w, layer, stride, name):
    M, d_ff = G.shape
    tc = _pick(d_ff, (1408, 512, 256, 128))
    halo = state.shape[0]
    return pl.pallas_call(
        functools.partial(_ffn_act_kernel, stride=stride),
        grid=(d_ff // tc,),
        in_specs=[pl.BlockSpec((M, tc), lambda c: (0, c)),
                  pl.BlockSpec((M, tc), lambda c: (0, c)),
                  pl.BlockSpec((halo, tc), lambda c: (0, c)),
                  pl.BlockSpec((None, FFN_CONV_W, tc), lambda c: (layer, 0, c))],
        out_specs=pl.BlockSpec((M, tc), lambda c: (0, c)),
        out_shape=jax.ShapeDtypeStruct((M, d_ff), BF16),
        scratch_shapes=[pltpu.VMEM((halo + M, tc), F32)],
        compiler_params=_cparams("parallel"),
        name=name,
    )(G, U, state, conv_w)


def kernel(x_prompt, x_sample, cache_k, cache_v, state_conv, state_hgrn, state_ffn_conv, page_table, hgrn_lower_bound, sb_bias, norm_mix_g, w_in, conv_w, conv_b, conv_ln_g, conv_ln_b, hgrn_norm_g, w_out_a, w_out_b, w_out_c, w_o, norm_ffn_g, w_gate, w_up, ffn_conv_w, w_down, norm_final_g):
    Bp, T, D = x_prompt.shape
    Bs, Ts, _ = x_sample.shape
    depth = w_in.shape[0]
    d_a = H_A * DH_A
    d_b = conv_w.shape[2]
    d_c = H_C * DK_C
    d_ff = w_gate.shape[2]
    off_glu = 3 * d_a
    off_f = off_glu + 2 * d_b
    off_gates = off_f + 4 * d_c
    tp = 16

    n_phys, page = cache_k.shape[1], cache_k.shape[2]
    ckt = cache_k.transpose(0, 1, 3, 4, 2).reshape(depth, n_phys, d_a, page)
    cvt = cache_v.transpose(0, 1, 3, 4, 2).reshape(depth, n_phys, d_a, page)
    xp = x_prompt.reshape(Bp * T, D)
    xs = x_sample.transpose(1, 0, 2).reshape(Ts * Bs, D)

    def to_batch_major(a):
        return a.reshape(Ts, Bs, a.shape[-1]).transpose(1, 0, 2)

    def to_time_major(a):
        return a.transpose(1, 0, 2).reshape(Ts * Bs, a.shape[-1])

    outs = {k: [] for k in ("kp", "vp", "ks", "vs", "cp", "cs", "hp", "hs", "fp", "fs")}
    for l in range(depth):
        tag = f"l{l}"
        (pp,) = norm_matmul(xp, norm_mix_g, [w_in], l, f"in_p_{tag}")
        oa = attn_prompt(pp, sb_bias, l, Bp, T, f"attn_p_{tag}")
        ob, tail = conv_module(pp, off_glu, d_b, None, conv_w, conv_b, conv_ln_g, conv_ln_b, l, T, 1,
                               f"conv_p_{tag}")
        oc, hst = hgrn_prompt(pp, off_f, hgrn_lower_bound, hgrn_norm_g, l, Bp, T, f"hgrn_p_{tag}")
        mg = merge_branches(oa, ob, oc, pp, off_gates, w_out_a, w_out_b, w_out_c, l, f"merge_p_{tag}")
        xp = matmul_residual(xp, mg, w_o, l, f"wo_p_{tag}")
        act, gtail = ffn_in_act(xp, norm_ffn_g, w_gate, w_up, ffn_conv_w, l, T, f"ffn_in_p_{tag}")
        xp = matmul_residual(xp, act, w_down, l, f"down_p_{tag}")
        outs["kp"].append(pp[:, d_a:2 * d_a].reshape(Bp, T, H_A, DH_A))
        outs["vp"].append(pp[:, 2 * d_a:3 * d_a].reshape(Bp, T, H_A, DH_A))
        outs["cp"].append(tail[:, tail.shape[1] - (CONV_W - 1):, :])
        outs["hp"].append(hst)
        gtail = gtail.reshape(Bp, gtail.shape[0] // Bp, gtail.shape[1], d_ff)[:, -1]
        outs["fp"].append(gtail[:, gtail.shape[1] - (FFN_CONV_W - 1):, :])

        (ps,) = norm_matmul(xs, norm_mix_g, [w_in], l, f"in_s_{tag}")
        qkv = to_batch_major(ps[:, :3 * d_a])
        q_s, k_s, v_s = qkv[..., :d_a], qkv[..., d_a:2 * d_a], qkv[..., 2 * d_a:]
        new_t = lambda a: jnp.pad(a.transpose(0, 2, 1), ((0, 0), (0, 0), (0, page - Ts)))
        oa = attn_sample(q_s, new_t(k_s), new_t(v_s), ckt, cvt, page_table, sb_bias, l, f"attn_s_{tag}")
        oa = to_time_major(oa).astype(BF16)
        cst =state_conv[l].transpose(1, 0, 2).reshape((CONV_W - 1) * Bs, d_b)
        ob, tail = conv_module(ps, off_glu, d_b, cst, conv_w, conv_b, conv_ln_g, conv_ln_b, l, Ts, Bs,
                               f"conv_s_{tag}")
        fiqg = to_batch_major(ps[:, off_f:off_f + 4 * d_c]).reshape(Bs, Ts, 4, d_c)
        fiqg = jnp.pad(fiqg.transpose(2, 0, 1, 3), ((0, 0), (0, 0), (0, tp - Ts), (0, 0)))
        oc, hst = hgrn_sample(fiqg, state_hgrn[l], hgrn_lower_bound, hgrn_norm_g, l, Ts, f"hgrn_s_{tag}")
        oc = to_time_major(oc[:, :Ts, :])
        mg = merge_branches(oa, ob, oc, ps, off_gates, w_out_a, w_out_b, w_out_c, l, f"merge_s_{tag}")
        xs = matmul_residual(xs, mg, w_o, l, f"wo_s_{tag}")
        gs, us = norm_matmul(xs, norm_ffn_g, [w_gate, w_up], l, f"ffn_in_s_{tag}")
        fst = state_ffn_conv[l].transpose(1, 0, 2).reshape((FFN_CONV_W - 1) * Bs, d_ff)
        act = ffn_act(gs, us, fst, ffn_conv_w, l, Bs, f"ffn_act_s_{tag}")
        xs = matmul_residual(xs, act, w_down, l, f"down_s_{tag}")
        outs["ks"].append(k_s.reshape(Bs, Ts, H_A, DH_A))
        outs["vs"].append(v_s.reshape(Bs, Ts, H_A, DH_A))
        outs["cs"].append(tail.reshape(CONV_W - 1, Bs, d_b).transpose(1, 0, 2))
        outs["hs"].append(hst)
        ext_g = jnp.concatenate([fst, gs], axis=0)
        outs["fs"].append(ext_g[ext_g.shape[0] - (FFN_CONV_W - 1) * Bs:]
                          .reshape(FFN_CONV_W - 1, Bs, d_ff).transpose(1, 0, 2))

    y_p = rmsnorm_rows(xp, norm_final_g, "final_p").reshape(Bp, T, D)
    y_s = to_batch_major(rmsnorm_rows(xs, norm_final_g, "final_s"))
    st = lambda k: jnp.stack(outs[k])
    return (y_p, y_s, st("kp"), st("vp"), st("ks"), st("vs"), st("cp"), st("cs"),
            st("hp"), st("hs"), st("fp"), st("fs"))
```

```python
import functools

import jax
import jax.numpy as jnp
from jax import lax
from jax.experimental import pallas as pl
from jax.experimental.pallas import tpu as pltpu

F32 = jnp.float32
BF16 = jnp.bfloat16
EPS = 1e-6

VMEM_LIMIT_BYTES = 56 * 1024 * 1024
SUBLANES = 8
LANES = 128

H_A = 16
DH_A = 64
H_C = 8
DK_C = 128
CONV_W = 31
FFN_CONV_W = 3
HGRN_CHUNK = 64
HGRN_SUB = 8
HGRN_HEADS_PER_STEP = 4
ATT_TQ = 256
ATT_TK = 256
ATT_HEADS_PER_STEP = 4


def _cparams(*sem):
    return pltpu.CompilerParams(dimension_semantics=sem, vmem_limit_bytes=VMEM_LIMIT_BYTES)


def _pick(n, prefs):
    for p in prefs:
        if n % p == 0:
            return p
    return n


def _log_sigmoid_parts(z):
    l = jnp.log(1.0 + jnp.exp(_neg_abs(z)))
    return jnp.minimum(z, 0.0) - l, -(jnp.maximum(z, 0.0) + l)


def _silu(x):
    return x * jax.nn.sigmoid(x)


def _cast_kernel(x_ref, o_ref):
    o_ref[...] = x_ref[...].astype(o_ref.dtype)


def cast_bf16(w, name):
    d, K, N = w.shape
    rows = d * K
    tr = _pick(rows, tuple(r for r in (1024, 512, 256, 128, 64, 32, 16) if r * N <= 2 * 1024 * 1024))
    out = pl.pallas_call(
        _cast_kernel,
        grid=(rows // tr,),
        in_specs=[pl.BlockSpec((tr, N), lambda i: (i, 0))],
        out_specs=pl.BlockSpec((tr, N), lambda i: (i, 0)),
        out_shape=jax.ShapeDtypeStruct((rows, N), BF16),
        compiler_params=_cparams("parallel"),
        name=name,
    )(w.reshape(rows, N))
    return out.reshape(d, K, N)


def _norm_mm_kernel(x_ref, g_ref, *refs, n_w):
    w_refs = refs[:n_w]
    o_refs = refs[n_w:2 * n_w]
    h_ref = refs[2 * n_w]

    @pl.when(pl.program_id(1) == 0)
    def _():
        xf = x_ref[...]
        r = lax.rsqrt(jnp.mean(xf * xf, axis=-1, keepdims=True) + EPS)
        h_ref[...] = ((xf * r) * g_ref[...]).astype(BF16)

    h = h_ref[...]
    for w_ref, o_ref in zip(w_refs, o_refs):
        o_ref[...] = jnp.dot(h, w_ref[...].astype(BF16), preferred_element_type=F32)


def norm_matmul(x, g, ws, layer, name):
    M, D = x.shape
    N = ws[0].shape[2]
    tm = _pick(M, (1024, 512, 256, 128))
    n_w = len(ws)
    tn = _pick(N, (1024, 512, 256, 128) if n_w == 1 else (512, 256, 128))
    g3 = g.reshape(g.shape[0], 1, D)
    outs = pl.pallas_call(
        functools.partial(_norm_mm_kernel, n_w=n_w),
        grid=(M // tm, N // tn),
        in_specs=[pl.BlockSpec((tm, D), lambda i, j: (i, 0)),
                  pl.BlockSpec((None, 1, D), lambda i, j: (layer, 0, 0))]
                 + [pl.BlockSpec((None, D, tn), lambda i, j: (layer, 0, j)) for _ in ws],
        out_specs=[pl.BlockSpec((tm, tn), lambda i, j: (i, j)) for _ in ws],
        out_shape=[jax.ShapeDtypeStruct((M, N), F32) for _ in ws],
        scratch_shapes=[pltpu.VMEM((tm, D), BF16)],
        compiler_params=_cparams("parallel", "arbitrary"),
        name=name,
    )(x, g3, *ws)
    return outs


def _rmsnorm_kernel(x_ref, g_ref, o_ref):
    xf = x_ref[...]
    r = lax.rsqrt(jnp.mean(xf * xf, axis=-1, keepdims=True) + EPS)
    o_ref[...] = (xf * r) * g_ref[...]


def rmsnorm_rows(x, g, name):
    M, D = x.shape
    tm = _pick(M, (512, 256, 128))
    return pl.pallas_call(
        _rmsnorm_kernel,
        grid=(M // tm,),
        in_specs=[pl.BlockSpec((tm, D), lambda i: (i, 0)),
                  pl.BlockSpec((1, D), lambda i: (0, 0))],
        out_specs=pl.BlockSpec((tm, D), lambda i: (i, 0)),
        out_shape=jax.ShapeDtypeStruct((M, D), F32),
        compiler_params=_cparams("parallel"),
        name=name,
    )(x, g.reshape(1, D))


def _mm_res_kernel(x_ref, a_ref, w_ref, o_ref):
    o_ref[...] = x_ref[...] + jnp.dot(a_ref[...], w_ref[...].astype(BF16),
                                      preferred_element_type=F32)


def matmul_residual(x, a, w, layer, name):
    M, N = x.shape
    K = a.shape[1]
    tm = _pick(M, (1024, 512, 256, 128))
    tn = _pick(N, (512, 256, 128) if K > 2048 else (1024, 512, 256, 128))
    return pl.pallas_call(
        _mm_res_kernel,
        grid=(M // tm, N // tn),
        in_specs=[pl.BlockSpec((tm, tn), lambda i, j: (i, j)),
                  pl.BlockSpec((tm, K), lambda i, j: (i, 0)),
                  pl.BlockSpec((None, K, tn), lambda i, j: (layer, 0, j))],
        out_specs=pl.BlockSpec((tm, tn), lambda i, j: (i, j)),
        out_shape=jax.ShapeDtypeStruct((M, N), F32),
        compiler_params=_cparams("parallel", "arbitrary"),
        name=name,
    )(x, a, w)


def _merge_kernel(oa_ref, ob_ref, oc_ref, g0_ref, g1_ref, g2_ref, wa_ref, wb_ref, wc_ref, o_ref):
    def branch(o_r, g_r, w_r):
        y = jnp.dot(o_r[...], w_r[...].astype(BF16), preferred_element_type=F32)
        return jax.nn.sigmoid(g_r[...]) * y

    m = branch(oa_ref, g0_ref, wa_ref) + branch(ob_ref, g1_ref, wb_ref) + branch(oc_ref, g2_ref, wc_ref)
    o_ref[...] = m.astype(BF16)


def merge_branches(oa, ob, oc, proj, gate_off, wa, wb, wc, layer, name):
    M, K = oa.shape
    N = wa.shape[2]
    tm = _pick(M, (1024, 512, 256, 128))
    tn = _pick(N, (512, 256, 128))
    gb = gate_off // tn
    nb = N // tn
    o_spec = pl.BlockSpec((tm, K), lambda i, j: (i, 0))
    w_spec = pl.BlockSpec((None, K, tn), lambda i, j: (layer, 0, j))
    return pl.pallas_call(
        _merge_kernel,
        grid=(M // tm, nb),
        in_specs=[o_spec, o_spec, o_spec,
                  pl.BlockSpec((tm, tn), lambda i, j: (i, gb + j)),
                  pl.BlockSpec((tm, tn), lambda i, j: (i, gb + nb + j)),
                  pl.BlockSpec((tm, tn), lambda i, j: (i, gb + 2 * nb + j)),
                  w_spec, w_spec, w_spec],
        out_specs=pl.BlockSpec((tm, tn), lambda i, j: (i, j)),
        out_shape=jax.ShapeDtypeStruct((M, N), BF16),
        compiler_params=_cparams("parallel", "arbitrary"),
        name=name,
    )(oa, ob, oc, proj, proj, proj, wa, wb, wc)


LOG2E = 1.4426950408889634


def _strict_upper2(n):
    r = lax.broadcasted_iota(jnp.int32, (2 * n, n), 0)
    c = lax.broadcasted_iota(jnp.int32, (2 * n, n), 1)
    r = jnp.where(r >= n, r - n, r)
    return jnp.where(r > c, 1.0, 0.0).astype(BF16)


def _neg_abs(x):
    bits = pltpu.bitcast(x, jnp.uint32) | jnp.uint32(0x80000000)
    return pltpu.bitcast(bits, F32)


MASKED_LOGIT = -1e30


def _sb_tiles(z2s, v_bfs, u2, crys, chained=False, v_transposed=False):
    staged = []
    for z2 in z2s:
        l2 = jnp.log2(1.0 + jnp.exp2(_neg_abs(z2)))
        log_beta = jnp.minimum(z2, 0.0) - l2
        neg_log_rem = jnp.maximum(z2, 0.0) + l2
        hi = neg_log_rem.astype(BF16)
        if u2.shape[0] == 2 * u2.shape[1]:
            lo = (neg_log_rem - hi.astype(F32)).astype(BF16)
            hi = jnp.concatenate([hi, lo], axis=1)
        suffix = jnp.dot(hi, u2, preferred_element_type=F32)
        staged.append((log_beta, suffix, jnp.sum(neg_log_rem, axis=-1, keepdims=True)))
    contribs, new_crys = [], []
    for n, ((log_beta, suffix, row_sum), v_bf) in enumerate(zip(staged, v_bfs)):
        cry = crys if chained else crys[n]
        w = jnp.exp2(log_beta - suffix - cry).astype(BF16)
        v_contract = 1 if v_transposed else 0
        contribs.append(lax.dot_general(w, v_bf, (((1,), (v_contract,)), ((), ())),
                                        preferred_element_type=F32))
        if chained:
            crys = cry + row_sum
        else:
            new_crys.append(cry + row_sum)
    return contribs, (crys if chained else new_crys)


def _attn_prompt_kernel(bias_ref, q_ref, k_ref, v_ref, o_ref, kb_ref, vb_ref, acc_ref, cry_ref, *,
                        layer, heads_per_blk):
    hp = pl.program_id(1)
    i = pl.program_id(2)
    tq = q_ref.shape[0]
    tk = ATT_TK
    dh = DH_A

    heads = range(heads_per_blk)

    @pl.when(i == 0)
    def _():
        lane = lax.broadcasted_iota(jnp.int32, (k_ref.shape[0], dh), 1)
        for hh in heads:
            b2 = jnp.full(lane.shape, bias_ref[layer * H_A + hp * heads_per_blk + hh] * LOG2E, F32)
            b_hi = b2.astype(BF16).astype(F32)
            extra = jnp.where(lane == 0, b_hi, jnp.where(lane == 1, b2 - b_hi, 0.0)).astype(BF16)
            kb_ref[hh] = jnp.concatenate([k_ref[:, hh * dh:(hh + 1) * dh].astype(BF16), extra], axis=1)
            vb_ref[hh] = v_ref[:, hh * dh:(hh + 1) * dh].astype(BF16)

    u2 = _strict_upper2(tk)[:tk]
    diag_mask = (lax.broadcasted_iota(jnp.int32, (tq, tk), 1)
                 < lax.broadcasted_iota(jnp.int32, (tq, tk), 0))
    ones2 = jnp.where(lax.broadcasted_iota(jnp.int32, (tq, dh), 1) < 2, 1.0, 0.0).astype(BF16)
    q2 = [jnp.concatenate([(q_ref[:, hh * dh:(hh + 1) * dh] * (dh ** -0.5 * LOG2E)).astype(BF16), ones2],
                          axis=1) for hh in heads]

    def tile(j, mask, first):
        rows = pl.ds(pl.multiple_of(j * tk, tk), tk)
        z2s = [lax.dot_general(q2[hh], kb_ref[hh, rows, :], (((1,), (1,)), ((), ())),
                               preferred_element_type=F32) for hh in heads]
        if mask is not None:
            z2s = [jnp.where(mask, z2, MASKED_LOGIT) for z2 in z2s]
        crys = [jnp.zeros((tq, 1), F32) if first else cry_ref[hh] for hh in heads]
        contribs, crys = _sb_tiles(z2s, [vb_ref[hh, rows, :] for hh in heads], u2, crys)
        for hh in heads:
            acc_ref[hh] = contribs[hh] if first else acc_ref[hh] + contribs[hh]
            cry_ref[hh] = crys[hh]

    tile(i, diag_mask, True)

    def body(jj, carry):
        tile(i - 1 - jj, None, False)
        return carry

    lax.fori_loop(0, i, body, 0)
    for hh in heads:
        o_ref[:, hh * dh:(hh + 1) * dh] = acc_ref[hh].astype(o_ref.dtype)


def attn_prompt(proj, sb_bias, layer, B, T, name):
    d_a = H_A * DH_A
    hpb = ATT_HEADS_PER_STEP
    wb = hpb * DH_A
    n_hp = H_A // hpb
    nq = T // ATT_TQ
    kv_blk = d_a // wb
    return pl.pallas_call(
        functools.partial(_attn_prompt_kernel, layer=layer, heads_per_blk=hpb),
        grid=(B, n_hp, nq),
        in_specs=[pl.BlockSpec(memory_space=pltpu.SMEM),
                  pl.BlockSpec((ATT_TQ, wb), lambda b, h, i: (b * nq + i, h)),
                  pl.BlockSpec((T, wb), lambda b, h, i: (b, kv_blk + h)),
                  pl.BlockSpec((T, wb), lambda b, h, i: (b, 2 * kv_blk + h))],
        out_specs=pl.BlockSpec((ATT_TQ, wb), lambda b, h, i: (b * nq + i, h)),
        out_shape=jax.ShapeDtypeStruct((B * T, d_a), BF16),
        scratch_shapes=[pltpu.VMEM((hpb, T, 2 * DH_A), BF16), pltpu.VMEM((hpb, T, DH_A), BF16),
                        pltpu.VMEM((hpb, ATT_TQ, DH_A), F32), pltpu.VMEM((hpb, ATT_TQ, 1), F32)],
        compiler_params=_cparams("parallel", "parallel", "arbitrary"),
        name=name,
    )(sb_bias.reshape(-1), proj, proj, proj)


def _attn_sample_kernel(pt_ref, q_ref, kn_ref, vn_ref, bias_ref, *refs, n_pg):
    del pt_ref
    k_refs = refs[:n_pg]
    v_refs = refs[n_pg:2 * n_pg]
    o_ref = refs[2 * n_pg]
    qbd_ref, acc_ref, cry_ref = refs[2 * n_pg + 1:]
    g = pl.program_id(1)
    ts, d_a = q_ref.shape
    rows = ts * H_A
    page = k_refs[0].shape[1]
    head_mask = (lax.broadcasted_iota(jnp.int32, (H_A, d_a), 1) // DH_A
                 == lax.broadcasted_iota(jnp.int32, (H_A, d_a), 0))
    bias2 = bias_ref[...] * LOG2E
    u2 = _strict_upper2(page)

    def logits(kt_ref):
        return jnp.dot(qbd_ref[...], kt_ref[...].astype(BF16), preferred_element_type=F32) + bias2

    @pl.when(g == 0)
    def _():
        q = q_ref[...] * (DH_A ** -0.5 * LOG2E)
        qbd_ref[...] = jnp.concatenate(
            [jnp.where(head_mask, jnp.broadcast_to(q[t:t + 1, :], (H_A, d_a)), 0.0) for t in range(ts)],
            axis=0).astype(BF16)
        t_of_row = lax.broadcasted_iota(jnp.int32, (rows, page), 0) // H_A
        mask = lax.broadcasted_iota(jnp.int32, (rows, page), 1) < t_of_row
        contribs, crys = _sb_tiles([jnp.where(mask, logits(kn_ref), MASKED_LOGIT)],
                                   [vn_ref[...].astype(BF16)], u2, [jnp.zeros((rows, 1), F32)],
                                   v_transposed=True)
        acc_ref[...] = contribs[0]
        cry_ref[...] = crys[0]

    contribs, cry = _sb_tiles([logits(k_refs[r]) for r in range(n_pg)],
                              [v_refs[r][...].astype(BF16) for r in range(n_pg)],
                              u2, cry_ref[...], chained=True, v_transposed=True)
    acc = acc_ref[...]
    for contrib in contribs:
        acc = acc + contrib
    acc_ref[...] = acc
    cry_ref[...] = cry

    @pl.when(g == pl.num_programs(1) - 1)
    def _():
        hm = head_mask.astype(F32)
        for t in range(ts):
            o_ref[t:t + 1, :] = jnp.sum(acc_ref[t * H_A:(t + 1) * H_A, :] * hm, axis=0, keepdims=True)


def attn_sample(q, k_new_t, v_new_t, cache_kt, cache_vt, page_table, sb_bias, layer, name):
    Bs, Ts, d_a = q.shape
    n_pages = page_table.shape[1]
    page = cache_kt.shape[3]
    n_pg = _pick(n_pages, (16, 8, 4, 2, 1))
    n_grp = n_pages // n_pg
    bias_col = jnp.tile(sb_bias[layer], Ts).reshape(Ts * H_A, 1)

    def page_spec(r):
        return pl.BlockSpec((None, None, d_a, page),
                            lambda b, g, pt: (layer, pt[b, n_pages - 1 - (g * n_pg + r)], 0, 0))

    new_spec = pl.BlockSpec((None, d_a, page), lambda b, g, pt: (b, 0, 0))
    grid_spec = pltpu.PrefetchScalarGridSpec(
        num_scalar_prefetch=1,
        grid=(Bs, n_grp),
        in_specs=[pl.BlockSpec((None, Ts, d_a), lambda b, g, pt: (b, 0, 0)), new_spec, new_spec,
                  pl.BlockSpec((Ts * H_A, 1), lambda b, g, pt: (0, 0))]
                 + [page_spec(r) for r in range(n_pg)] + [page_spec(r) for r in range(n_pg)],
        out_specs=pl.BlockSpec((None, Ts, d_a), lambda b, g, pt: (b, 0, 0)),
        scratch_shapes=[pltpu.VMEM((Ts * H_A, d_a), BF16),
                        pltpu.VMEM((Ts * H_A, d_a), F32),
                        pltpu.VMEM((Ts * H_A, 1), F32)],
    )
    return pl.pallas_call(
        functools.partial(_attn_sample_kernel, n_pg=n_pg),
        grid_spec=grid_spec,
        out_shape=jax.ShapeDtypeStruct((Bs, Ts, d_a), F32),
        compiler_params=_cparams("parallel", "arbitrary"),
        name=name,
    )(page_table, q, k_new_t, v_new_t, bias_col, *([cache_kt] * n_pg), *([cache_vt] * n_pg))


def _conv_kernel(a_ref, b_ref, ha_ref, hb_ref, w_ref, cb_ref, lg_ref, lb_ref, y_ref, tail_ref,
                 ubuf, cbuf, sbuf, *, stride, halo, tiles_per_seq, halo_is_state):
    tt = a_ref.shape[0]
    d_b = a_ref.shape[1]
    if halo_is_state:
        ubuf[0:halo, :] = ha_ref[...]
    else:
        first = (pl.program_id(0) % tiles_per_seq) == 0
        uh = ha_ref[...] * jax.nn.sigmoid(hb_ref[...])
        ubuf[0:halo, :] = jnp.where(first, 0.0, uh)
    ubuf[halo:halo + tt, :] = a_ref[...] * jax.nn.sigmoid(b_ref[...])

    offs = [halo - (CONV_W - 1) * stride + j * stride for j in range(CONV_W)]
    groups = {}
    for j, off in enumerate(offs):
        groups.setdefault(off % SUBLANES, []).append(j)
    rc = min(tt, 64)
    for c in range(d_b // LANES):
        cols = slice(c * LANES, (c + 1) * LANES)
        for r, taps in groups.items():
            if r:
                span = max(offs[j] for j in taps) - r + tt
                sbuf[r, 0:span, :] = ubuf[r:r + span, cols]
        for r0 in range(0, tt, rc):
            acc = jnp.zeros((rc, LANES), F32)
            for r, taps in groups.items():
                for j in taps:
                    a8 = offs[j] - r + r0
                    tap = sbuf[r, a8:a8 + rc, :] if r else ubuf[a8:a8 + rc, cols]
                    acc = acc + w_ref[j:j + 1, cols] * tap
            cbuf[r0:r0 + rc, cols] = acc

    ln_rows = min(tt, 32)
    for r0 in range(0, tt, ln_rows):
        cv = cbuf[r0:r0 + ln_rows, :] + cb_ref[...]
        mu = jnp.mean(cv, axis=-1, keepdims=True)
        d = cv - mu
        var = jnp.mean(d * d, axis=-1, keepdims=True)
        y = (d * lax.rsqrt(var + EPS)) * lg_ref[...] + lb_ref[...]
        y_ref[r0:r0 + ln_rows, :] = _silu(y).astype(y_ref.dtype)
    n_tail = tail_ref.shape[0]
    tail_ref[...] = ubuf[halo + tt - n_tail:halo + tt, :]


def conv_module(proj, glu_off, d_b, state, conv_w, conv_b, ln_g, ln_b, layer, seq_len, stride, name):
    M = proj.shape[0]
    ca = glu_off // d_b
    w = conv_w[layer]
    vecs = [conv_b[layer].reshape(1, d_b), ln_g[layer].reshape(1, d_b), ln_b[layer].reshape(1, d_b)]
    vec_spec = pl.BlockSpec((1, d_b), lambda i: (0, 0))
    if state is None:
        tt = _pick(seq_len, (256, 128))
        halo = 32
        tps = seq_len // tt
        n_seq = M // seq_len
        hblk = tt // halo
        halo_specs = [pl.BlockSpec((halo, d_b), lambda i: (jnp.maximum(i * hblk - 1, 0), ca)),
                      pl.BlockSpec((halo, d_b), lambda i: (jnp.maximum(i * hblk - 1, 0), ca + 1))]
        halo_args = [proj, proj]
        n_tail = halo
        tail_shape = (n_seq, n_tail, d_b)
        tail_spec = pl.BlockSpec((None, n_tail, d_b), lambda i: (i // tps, 0, 0))
        grid = (M // tt,)
    else:
        tt = M
        halo = state.shape[0]
        tps = 1
        halo_specs = [pl.BlockSpec((halo, d_b), lambda i: (0, 0)),
                      pl.BlockSpec((halo, d_b), lambda i: (0, 0))]
        halo_args = [state, state]
        n_tail = halo
        tail_shape = (n_tail, d_b)
        tail_spec = pl.BlockSpec((n_tail, d_b), lambda i: (0, 0))
        grid = (1,)
    return pl.pallas_call(
        functools.partial(_conv_kernel, stride=stride, halo=halo, tiles_per_seq=tps,
                          halo_is_state=state is not None),
        grid=grid,
        in_specs=[pl.BlockSpec((tt, d_b), lambda i: (i, ca)),
                  pl.BlockSpec((tt, d_b), lambda i: (i, ca + 1))] + halo_specs
                 + [pl.BlockSpec((CONV_W, d_b), lambda i: (0, 0)), vec_spec, vec_spec, vec_spec],
        out_specs=[pl.BlockSpec((tt, d_b), lambda i: (i, 0)), tail_spec],
        out_shape=[jax.ShapeDtypeStruct((M, d_b), BF16), jax.ShapeDtypeStruct(tail_shape, F32)],
        scratch_shapes=[pltpu.VMEM((halo + tt, d_b), F32), pltpu.VMEM((tt, d_b), F32),
                        pltpu.VMEM((SUBLANES, halo + tt, LANES), F32)],
        compiler_params=_cparams("arbitrary"),
        name=name,
    )(proj, proj, *halo_args, w, *vecs)


def _hgrn_lower_bound(lbraw, layer):
    e = jnp.exp(lbraw - jnp.max(lbraw, axis=0, keepdims=True))
    p = e / jnp.sum(e, axis=0, keepdims=True)
    lb = jnp.zeros((1, lbraw.shape[1]), F32)
    for r in range(1, layer + 1):
        lb = lb + p[r:r + 1, :]
    return lb


def _cumsum_rows(x):
    n = x.shape[0]
    r = lax.broadcasted_iota(jnp.int32, (n, n), 0)
    c = lax.broadcasted_iota(jnp.int32, (n, n), 1)
    tri = jnp.where(r >= c, 1.0, 0.0).astype(BF16)
    hi = x.astype(BF16)
    r1 = x - hi.astype(F32)
    mid = r1.astype(BF16)
    lo = (r1 - mid.astype(F32)).astype(BF16)
    parts = jnp.dot(tri, jnp.concatenate([hi, mid, lo], axis=1), preferred_element_type=F32)
    w = x.shape[1]
    return parts[:, :w] + parts[:, w:2 * w] + parts[:, 2 * w:]


def _hgrn_chunks(heads, *, sub, n_valid):
    c = heads[0][0].shape[0]
    nt = (((1,), (1,)), ((), ()))
    n_sub = -(-min(c, n_valid) // sub)
    row_c = lax.broadcasted_iota(jnp.int32, (c, 1), 0)
    row_l = lax.broadcasted_iota(jnp.int32, (sub, 1), 0)

    kg = []
    for zf, iv, q, gate, lb, gn, st in heads:
        log_sig, log_sig_neg = _log_sigmoid_parts(zf)
        a_ = jnp.log(lb)
        b_ = jnp.log1p(-lb) + log_sig
        log_f = jnp.maximum(a_, b_) + jnp.log(1.0 + jnp.exp(_neg_abs(a_ - b_)))
        k = (1.0 - lb) * jnp.exp(log_sig_neg)
        if n_valid < c:
            valid = lax.broadcasted_iota(jnp.int32, zf.shape, 0) < n_valid
            log_f = jnp.where(valid, log_f, 0.0)
            k = jnp.where(valid, k, 0.0)
        kg.append((k, _cumsum_rows(log_f)))

    o_inter = [lax.dot_general((q * jnp.exp(G)).astype(BF16), st.astype(BF16), nt,
                               preferred_element_type=F32)
               for (zf, iv, q, gate, lb, gn, st), (k, G) in zip(heads, kg)]

    scores = []
    for (zf, iv, q, gate, lb, gn, st), (k, G) in zip(heads, kg):
        per = []
        for I in range(1, n_sub):
            lo = I * sub
            ref = G[lo - 1:lo]
            qd = q[lo:lo + sub] * jnp.exp(G[lo:lo + sub] - ref)
            kd = jnp.where(row_c < lo, k * jnp.exp(jnp.minimum(ref - G, 0.0)), 0.0)
            per.append(lax.dot_general(qd.astype(BF16), kd.astype(BF16), nt, preferred_element_type=F32))
        scores.append(per)

    diag = []
    for (zf, iv, q, gate, lb, gn, st), (k, G) in zip(heads, kg):
        per = []
        for I in range(n_sub):
            lo = I * sub
            qI, GI = q[lo:lo + sub], G[lo:lo + sub]
            o = jnp.zeros((sub, iv.shape[1]), F32)
            for s in range(min(sub, n_valid - lo)):
                sg = lo + s
                e = jnp.exp(jnp.minimum(GI - G[sg:sg + 1], 0.0))
                col = jnp.sum(qI * (k[sg:sg + 1] * e), axis=-1, keepdims=True)
                o = o + jnp.where(row_l >= s, col, 0.0) * iv[sg:sg + 1]
            per.append(o)
        diag.append(per)

    results = []
    for n, ((zf, iv, q, gate, lb, gn, st), (k, G)) in enumerate(zip(heads, kg)):
        v_bf = iv.astype(BF16)
        outs = []
        for I in range(c // sub):
            lo = I * sub
            if I >= n_sub:
                outs.append(jnp.zeros((sub, iv.shape[1]), F32))
                continue
            o = o_inter[n][lo:lo + sub] + diag[n][I]
            if I > 0:
                o = o + jnp.dot(scores[n][I - 1].astype(BF16), v_bf, preferred_element_type=F32)
            outs.append(o)
        o = jnp.concatenate(outs, axis=0) if len(outs) > 1 else outs[0]
        g_end = G[c - 1:c]
        kdec = k * jnp.exp(g_end - G)
        st_new = st * jnp.exp(g_end) + lax.dot_general(v_bf, kdec.astype(BF16), (((0,), (0,)), ((), ())),
                                                      preferred_element_type=F32)
        r = lax.rsqrt(jnp.mean(o * o, axis=-1, keepdims=True) + EPS)
        results.append((((o * r) * gn) * _silu(gate), st_new))
    return results


def _hgrn_prompt_kernel(f_ref, i_ref, q_ref, g_ref, lb_ref, gn_ref, y_ref, s_ref, st_ref, *, layer, nh):
    tt = f_ref.shape[0]
    c = HGRN_CHUNK
    n = DK_C
    ti = pl.program_id(2)
    lbs = _hgrn_lower_bound(lb_ref[...], layer)
    gns = gn_ref[...]

    @pl.when(ti == 0)
    def _():
        st_ref[...] = jnp.zeros_like(st_ref)

    def body(ci, carry):
        rows = pl.ds(pl.multiple_of(ci * c, c), c)
        heads = []
        for h in range(nh):
            cols = slice(h * n, (h + 1) * n)
            heads.append((f_ref[rows, cols], i_ref[rows, cols], q_ref[rows, cols], g_ref[rows, cols],
                          lbs[:, cols], gns[:, cols], st_ref[h]))
        for h, (y, st_new) in enumerate(_hgrn_chunks(heads, sub=HGRN_SUB, n_valid=c)):
            y_ref[rows, h * n:(h + 1) * n] = y.astype(y_ref.dtype)
            st_ref[h] = st_new
        return carry

    lax.fori_loop(0, tt // c, body, 0)

    @pl.when(ti == pl.num_programs(2) - 1)
    def _():
        for h in range(nh):
            s_ref[h] = st_ref[h].T


def hgrn_prompt(proj, f_off, lb_raw, norm_g, layer, B, T, name):
    d_c = H_C * DK_C
    nh = HGRN_HEADS_PER_STEP
    wb = nh * DK_C
    n_hg = H_C // nh
    tt = _pick(T, (1024, 512, 256, 128, 64))
    nt = T // tt
    fb = f_off // wb

    def col(k):
        return pl.BlockSpec((tt, wb), lambda b, h, t: (b * nt + t, fb + k * n_hg + h))

    return pl.pallas_call(
        functools.partial(_hgrn_prompt_kernel, layer=layer, nh=nh),
        grid=(B, n_hg, nt),
        in_specs=[col(0), col(1), col(2), col(3),
                  pl.BlockSpec((lb_raw.shape[0], wb), lambda b, h, t: (0, h)),
                  pl.BlockSpec((1, wb), lambda b, h, t: (0, h))],
        out_specs=[pl.BlockSpec((tt, wb), lambda b, h, t: (b * nt + t, h)),
                   pl.BlockSpec((None, nh, DK_C, DK_C), lambda b, h, t: (b, h, 0, 0))],
        out_shape=[jax.ShapeDtypeStruct((B * T, d_c), BF16),
                   jax.ShapeDtypeStruct((B, H_C, DK_C, DK_C), F32)],
        scratch_shapes=[pltpu.VMEM((nh, DK_C, DK_C), F32)],
        compiler_params=_cparams("parallel", "parallel", "arbitrary"),
        name=name,
    )(proj, proj, proj, proj, lb_raw, norm_g[layer].reshape(1, d_c))


def _hgrn_sample_kernel(f_ref, i_ref, q_ref, g_ref, lb_ref, gn_ref, s0_ref, y_ref, s_ref, *, layer, ts, nh):
    n = DK_C
    lbs = _hgrn_lower_bound(lb_ref[...], layer)
    gns = gn_ref[...]
    heads = []
    for h in range(nh):
        cols = slice(h * n, (h + 1) * n)
        heads.append((f_ref[:, cols], i_ref[:, cols], q_ref[:, cols], g_ref[:, cols],
                      lbs[:, cols], gns[:, cols], s0_ref[h].T))
    for h, (y, st_new) in enumerate(_hgrn_chunks(heads, sub=f_ref.shape[0], n_valid=ts)):
        y_ref[:, h * n:(h + 1) * n] = y.astype(y_ref.dtype)
        s_ref[h] = st_new.T


def hgrn_sample(fiqg, state, lb_raw, norm_g, layer, ts, name):
    _, Bs, tp, d_c = fiqg.shape
    nh = H_C
    wb = nh * DK_C

    def col(k):
        return pl.BlockSpec((None, None, tp, wb), lambda b, h: (k, b, 0, h))

    st_spec = pl.BlockSpec((None, nh, DK_C, DK_C), lambda b, h: (b, h, 0, 0))
    return pl.pallas_call(
        functools.partial(_hgrn_sample_kernel, layer=layer, ts=ts, nh=nh),
        grid=(Bs, H_C // nh),
        in_specs=[col(0), col(1), col(2), col(3),
                  pl.BlockSpec((lb_raw.shape[0], wb), lambda b, h: (0, h)),
                  pl.BlockSpec((1, wb), lambda b, h: (0, h)),
                  st_spec],
        out_specs=[pl.BlockSpec((None, tp, wb), lambda b, h: (b, 0, h)), st_spec],
        out_shape=[jax.ShapeDtypeStruct((Bs, tp, d_c), BF16),
                   jax.ShapeDtypeStruct(state.shape, F32)],
        compiler_params=_cparams("parallel", "parallel"),
        name=name,
    )(fiqg, fiqg, fiqg, fiqg, lb_raw, norm_g[layer].reshape(1, d_c), state)


def _ffn_act_kernel(g_ref, u_ref, h_ref, w_ref, o_ref, gbuf, *, stride):
    tt = g_ref.shape[0]
    halo = h_ref.shape[0]
    gbuf[0:halo, :] = h_ref[...]
    gbuf[halo:halo + tt, :] = g_ref[...]
    a = w_ref[FFN_CONV_W - 1:FFN_CONV_W, :] * g_ref[...]
    for j in range(FFN_CONV_W - 1):
        s = halo - (FFN_CONV_W - 1 - j) * stride
        a = a + w_ref[j:j + 1, :] * gbuf[s:s + tt, :]
    o_ref[...] = (_silu(a) * u_ref[...]).astype(o_ref.dtype)


def _ffn_in_act_kernel(x_ref, g_ref, wg_ref, wu_ref, cw_ref, act_ref, tail_ref, h_ref, gbuf, carry_ref, *,
                       tiles_per_seq):
    i = pl.program_id(0)
    j = pl.program_id(1)
    tm = act_ref.shape[0]
    halo = tail_ref.shape[0]

    @pl.when(j == 0)
    def _():
        xf = x_ref[...]
        r = lax.rsqrt(jnp.mean(xf * xf, axis=-1, keepdims=True) + EPS)
        h_ref[...] = ((xf * r) * g_ref[...]).astype(BF16)

    h = h_ref[...]
    gate = jnp.dot(h, wg_ref[...].astype(BF16), preferred_element_type=F32)
    up = jnp.dot(h, wu_ref[...].astype(BF16), preferred_element_type=F32)
    first = (i % tiles_per_seq) == 0
    gbuf[0:halo, :] = jnp.where(first, 0.0, carry_ref[j])
    gbuf[halo:halo + tm, :] = gate
    a = cw_ref[FFN_CONV_W - 1:FFN_CONV_W, :] * gate
    for k in range(FFN_CONV_W - 1):
        s = halo - (FFN_CONV_W - 1 - k)
        a = a + cw_ref[k:k + 1, :] * gbuf[s:s + tm, :]
    act_ref[...] = (_silu(a) * up).astype(act_ref.dtype)
    last = gbuf[tm:tm + halo, :]
    carry_ref[j] = last
    tail_ref[...] = last


def ffn_in_act(x, g, w_gate, w_up, conv_w, layer, seq_len, name):
    M, D = x.shape
    N = w_gate.shape[2]
    tm = _pick(seq_len, (1024, 512, 256, 128))
    tn = _pick(N, (512, 256, 128))
    tps = seq_len // tm
    halo = SUBLANES
    g3 = g.reshape(g.shape[0], 1, D)
    w_spec = pl.BlockSpec((None, D, tn), lambda i, j: (layer, 0, j))
    return pl.pallas_call(
        functools.partial(_ffn_in_act_kernel, tiles_per_seq=tps),
        grid=(M // tm, N // tn),
        in_specs=[pl.BlockSpec((tm, D), lambda i, j: (i, 0)),
                  pl.BlockSpec((None, 1, D), lambda i, j: (layer, 0, 0)),
                  w_spec, w_spec,
                  pl.BlockSpec((None, FFN_CONV_W, tn), lambda i, j: (layer, 0, j))],
        out_specs=[pl.BlockSpec((tm, tn), lambda i, j: (i, j)),
                   pl.BlockSpec((None, halo, tn), lambda i, j: (i, 0, j))],
        out_shape=[jax.ShapeDtypeStruct((M, N), BF16),
                   jax.ShapeDtypeStruct((M // tm, halo, N), F32)],
        scratch_shapes=[pltpu.VMEM((tm, D), BF16), pltpu.VMEM((halo + tm, tn), F32),
                        pltpu.VMEM((N // tn, halo, tn), F32)],
        compiler_params=_cparams("arbitrary", "arbitrary"),
        name=name,
    )(x, g3, w_gate, w_up, conv_w)


def ffn_act(G, U, state, conv_w, layer, stride, name):
    M, d_ff = G.shape
    tc = _pick(d_ff, (1408, 512, 256, 128))
    halo = state.shape[0]
    return pl.pallas_call(
        functools.partial(_ffn_act_kernel, stride=stride),
        grid=(d_ff // tc,),
        in_specs=[pl.BlockSpec((M, tc), lambda c: (0, c)),
                  pl.BlockSpec((M, tc), lambda c: (0, c)),
                  pl.BlockSpec((halo, tc), lambda c: (0, c)),
                  pl.BlockSpec((None, FFN_CONV_W, tc), lambda c: (layer, 0, c))],
        out_specs=pl.BlockSpec((M, tc), lambda c: (0, c)),
        out_shape=jax.ShapeDtypeStruct((M, d_ff), BF16),
        scratch_shapes=[pltpu.VMEM((halo + M, tc), F32)],
        compiler_params=_cparams("parallel"),
        name=name,
    )(G, U, state, conv_w)


def kernel(x_prompt, x_sample, cache_k, cache_v, state_conv, state_hgrn, state_ffn_conv, page_table, hgrn_lower_bound, sb_bias, norm_mix_g, w_in, conv_w, conv_b, conv_ln_g, conv_ln_b, hgrn_norm_g, w_out_a, w_out_b, w_out_c, w_o, norm_ffn_g, w_gate, w_up, ffn_conv_w, w_down, norm_final_g):
    Bp, T, D = x_prompt.shape
    Bs, Ts, _ = x_sample.shape
    depth = w_in.shape[0]
    d_a = H_A * DH_A
    d_b = conv_w.shape[2]
    d_c = H_C * DK_C
    d_ff = w_gate.shape[2]
    off_glu = 3 * d_a
    off_f = off_glu + 2 * d_b
    off_gates = off_f + 4 * d_c
    tp = 16

    n_phys, page = cache_k.shape[1], cache_k.shape[2]
    ckt = cache_k.transpose(0, 1, 3, 4, 2).reshape(depth, n_phys, d_a, page)
    cvt = cache_v.transpose(0, 1, 3, 4, 2).reshape(depth, n_phys, d_a, page)
    xp = x_prompt.reshape(Bp * T, D)
    xs = x_sample.transpose(1, 0, 2).reshape(Ts * Bs, D)

    def to_batch_major(a):
        return a.reshape(Ts, Bs, a.shape[-1]).transpose(1, 0, 2)

    def to_time_major(a):
        return a.transpose(1, 0, 2).reshape(Ts * Bs, a.shape[-1])

    w_in, w_out_a, w_out_b, w_out_c, w_o, w_gate, w_up, w_down = [
        cast_bf16(w, f"cast_{n}") for n, w in enumerate(
            (w_in, w_out_a, w_out_b, w_out_c, w_o, w_gate, w_up, w_down))]

    outs = {k: [] for k in ("kp", "vp", "ks", "vs", "cp", "cs", "hp", "hs", "fp", "fs")}
    for l in range(depth):
        tag = f"l{l}"
        (pp,) = norm_matmul(xp, norm_mix_g, [w_in], l, f"in_p_{tag}")
        oa = attn_prompt(pp, sb_bias, l, Bp, T, f"attn_p_{tag}")
        ob, tail = conv_module(pp, off_glu, d_b, None, conv_w, conv_b, conv_ln_g, conv_ln_b, l, T, 1,
                               f"conv_p_{tag}")
        oc, hst = hgrn_prompt(pp, off_f, hgrn_lower_bound, hgrn_norm_g, l, Bp, T, f"hgrn_p_{tag}")
        mg = merge_branches(oa, ob, oc, pp, off_gates, w_out_a, w_out_b, w_out_c, l, f"merge_p_{tag}")
        xp = matmul_residual(xp, mg, w_o, l, f"wo_p_{tag}")
        act, gtail = ffn_in_act(xp, norm_ffn_g, w_gate, w_up, ffn_conv_w, l, T, f"ffn_in_p_{tag}")
        xp = matmul_residual(xp, act, w_down, l, f"down_p_{tag}")
        outs["kp"].append(pp[:, d_a:2 * d_a].reshape(Bp, T, H_A, DH_A))
        outs["vp"].append(pp[:, 2 * d_a:3 * d_a].reshape(Bp, T, H_A, DH_A))
        outs["cp"].append(tail[:, tail.shape[1] - (CONV_W - 1):, :])
        outs["hp"].append(hst)
        gtail = gtail.reshape(Bp, gtail.shape[0] // Bp, gtail.shape[1], d_ff)[:, -1]
        outs["fp"].append(gtail[:, gtail.shape[1] - (FFN_CONV_W - 1):, :])

        (ps,) = norm_matmul(xs, norm_mix_g, [w_in], l, f"in_s_{tag}")
        qkv = to_batch_major(ps[:, :3 * d_a])
        q_s, k_s, v_s = qkv[..., :d_a], qkv[..., d_a:2 * d_a], qkv[..., 2 * d_a:]
        new_t = lambda a: jnp.pad(a.transpose(0, 2, 1), ((0, 0), (0, 0), (0, page - Ts)))
        oa = attn_sample(q_s, new_t(k_s), new_t(v_s), ckt, cvt, page_table, sb_bias, l, f"attn_s_{tag}")
        oa = to_time_major(oa).astype(BF16)
        cst =state_conv[l].transpose(1, 0, 2).reshape((CONV_W - 1) * Bs, d_b)
        ob, tail = conv_module(ps, off_glu, d_b, cst, conv_w, conv_b, conv_ln_g, conv_ln_b, l, Ts, Bs,
                               f"conv_s_{tag}")
        fiqg = to_batch_major(ps[:, off_f:off_f + 4 * d_c]).reshape(Bs, Ts, 4, d_c)
        fiqg = jnp.pad(fiqg.transpose(2, 0, 1, 3), ((0, 0), (0, 0), (0, tp - Ts), (0, 0)))
        oc, hst = hgrn_sample(fiqg, state_hgrn[l], hgrn_lower_bound, hgrn_norm_g, l, Ts, f"hgrn_s_{tag}")
        oc = to_time_major(oc[:, :Ts, :])
        mg = merge_branches(oa, ob, oc, ps, off_gates, w_out_a, w_out_b, w_out_c, l, f"merge_s_{tag}")
        xs = matmul_residual(xs, mg, w_o, l, f"wo_s_{tag}")
        gs, us = norm_matmul(xs, norm_ffn_g, [w_gate, w_up], l, f"ffn_in_s_{tag}")
        fst = state_ffn_conv[l].transpose(1, 0, 2).reshape((FFN_CONV_W - 1) * Bs, d_ff)
        act = ffn_act(gs, us, fst, ffn_conv_w, l, Bs, f"ffn_act_s_{tag}")
        xs = matmul_residual(xs, act, w_down, l, f"down_s_{tag}")
        outs["ks"].append(k_s.reshape(Bs, Ts, H_A, DH_A))
        outs["vs"].append(v_s.reshape(Bs, Ts, H_A, DH_A))
        outs["cs"].append(tail.reshape(CONV_W - 1, Bs, d_b).transpose(1, 0, 2))
        outs["hs"].append(hst)
        ext_g = jnp.concatenate([fst, gs], axis=0)
        outs["fs"].append(ext_g[ext_g.shape[0] - (FFN_CONV_W - 1) * Bs:]
                          .reshape(FFN_CONV_W - 1, Bs, d_ff).transpose(1, 0, 2))

    y_p = rmsnorm_rows(xp, norm_final_g, "final_p").reshape(Bp, T, D)
    y_s = to_batch_major(rmsnorm_rows(xs, norm_final_g, "final_s"))
    st = lambda k: jnp.stack(outs[k])
    return (y_p, y_s, st("kp"), st("vp"), st("ks"), st("vs"), st("cp"), st("cs"),
            st("hp"), st("hs"), st("fp"), st("fs"))
```

```python
import functools

import jax
import jax.numpy as jnp
from jax import lax
from jax.experimental import pallas as pl
from jax.experimental.pallas import tpu as pltpu

F32 = jnp.float32
BF16 = jnp.bfloat16
EPS = 1e-6

VMEM_LIMIT_BYTES = 56 * 1024 * 1024
SUBLANES = 8
LANES = 128

H_A = 16
DH_A = 64
H_C = 8
DK_C = 128
CONV_W = 31
FFN_CONV_W = 3
HGRN_CHUNK = 64
HGRN_SUB = 8
HGRN_HEADS_PER_STEP = 4
ATT_TQ = 256
ATT_TK = 256
ATT_HEADS_PER_STEP = 4


def _cparams(*sem):
    return pltpu.CompilerParams(dimension_semantics=sem, vmem_limit_bytes=VMEM_LIMIT_BYTES)


def _pick(n, prefs):
    for p in prefs:
        if n % p == 0:
            return p
    return n


def _log_sigmoid_parts(z):
    l = jnp.log(1.0 + jnp.exp(_neg_abs(z)))
    return jnp.minimum(z, 0.0) - l, -(jnp.maximum(z, 0.0) + l)


def _silu(x):
    return x * jax.nn.sigmoid(x)


def _cast_kernel(x_ref, o_ref):
    o_ref[...] = x_ref[...].astype(o_ref.dtype)


def cast_bf16(w, name):
    d, K, N = w.shape
    rows = d * K
    tr = _pick(rows, tuple(r for r in (1024, 512, 256, 128, 64, 32, 16) if r * N <= 2 * 1024 * 1024))
    out = pl.pallas_call(
        _cast_kernel,
        grid=(rows // tr,),
        in_specs=[pl.BlockSpec((tr, N), lambda i: (i, 0))],
        out_specs=pl.BlockSpec((tr, N), lambda i: (i, 0)),
        out_shape=jax.ShapeDtypeStruct((rows, N), BF16),
        compiler_params=_cparams("parallel"),
        name=name,
    )(w.reshape(rows, N))
    return out.reshape(d, K, N)


def _transpose_kernel(x_ref, o_ref):
    o_ref[...] = x_ref[...].T


def transpose_cols(x, col_off, width, B, T, name):
    tt = _pick(T, (512, 256, 128))
    nt = T // tt
    cb = col_off // width
    return pl.pallas_call(
        _transpose_kernel,
        grid=(B, nt),
        in_specs=[pl.BlockSpec((tt, width), lambda b, t: (b * nt + t, cb))],
        out_specs=pl.BlockSpec((None, width, tt), lambda b, t: (b, 0, t)),
        out_shape=jax.ShapeDtypeStruct((B, width, T), x.dtype),
        compiler_params=_cparams("parallel", "parallel"),
        name=name,
    )(x)


def _norm_mm_kernel(x_ref, g_ref, *refs, n_w):
    w_refs = refs[:n_w]
    o_refs = refs[n_w:2 * n_w]
    h_ref = refs[2 * n_w]

    @pl.when(pl.program_id(1) == 0)
    def _():
        xf = x_ref[...]
        r = lax.rsqrt(jnp.mean(xf * xf, axis=-1, keepdims=True) + EPS)
        h_ref[...] = ((xf * r) * g_ref[...]).astype(BF16)

    h = h_ref[...]
    for w_ref, o_ref in zip(w_refs, o_refs):
        o_ref[...] = jnp.dot(h, w_ref[...].astype(BF16), preferred_element_type=F32)


def norm_matmul(x, g, ws, layer, name):
    M, D = x.shape
    N = ws[0].shape[2]
    tm = _pick(M, (1024, 512, 256, 128))
    n_w = len(ws)
    tn = _pick(N, (1024, 512, 256, 128) if n_w == 1 else (512, 256, 128))
    g3 = g.reshape(g.shape[0], 1, D)
    outs = pl.pallas_call(
        functools.partial(_norm_mm_kernel, n_w=n_w),
        grid=(M // tm, N // tn),
        in_specs=[pl.BlockSpec((tm, D), lambda i, j: (i, 0)),
                  pl.BlockSpec((None, 1, D), lambda i, j: (layer, 0, 0))]
                 + [pl.BlockSpec((None, D, tn), lambda i, j: (layer, 0, j)) for _ in ws],
        out_specs=[pl.BlockSpec((tm, tn), lambda i, j: (i, j)) for _ in ws],
        out_shape=[jax.ShapeDtypeStruct((M, N), F32) for _ in ws],
        scratch_shapes=[pltpu.VMEM((tm, D), BF16)],
        compiler_params=_cparams("parallel", "arbitrary"),
        name=name,
    )(x, g3, *ws)
    return outs


def _rmsnorm_kernel(x_ref, g_ref, o_ref):
    xf = x_ref[...]
    r = lax.rsqrt(jnp.mean(xf * xf, axis=-1, keepdims=True) + EPS)
    o_ref[...] = (xf * r) * g_ref[...]


def rmsnorm_rows(x, g, name):
    M, D = x.shape
    tm = _pick(M, (512, 256, 128))
    return pl.pallas_call(
        _rmsnorm_kernel,
        grid=(M // tm,),
        in_specs=[pl.BlockSpec((tm, D), lambda i: (i, 0)),
                  pl.BlockSpec((1, D), lambda i: (0, 0))],
        out_specs=pl.BlockSpec((tm, D), lambda i: (i, 0)),
        out_shape=jax.ShapeDtypeStruct((M, D), F32),
        compiler_params=_cparams("parallel"),
        name=name,
    )(x, g.reshape(1, D))


def _mm_res_kernel(x_ref, a_ref, w_ref, o_ref):
    o_ref[...] = x_ref[...] + jnp.dot(a_ref[...], w_ref[...].astype(BF16),
                                      preferred_element_type=F32)


def matmul_residual(x, a, w, layer, name):
    M, N = x.shape
    K = a.shape[1]
    tm = _pick(M, (1024, 512, 256, 128))
    tn = _pick(N, (512, 256, 128) if K > 2048 else (1024, 512, 256, 128))
    return pl.pallas_call(
        _mm_res_kernel,
        grid=(M // tm, N // tn),
        in_specs=[pl.BlockSpec((tm, tn), lambda i, j: (i, j)),
                  pl.BlockSpec((tm, K), lambda i, j: (i, 0)),
                  pl.BlockSpec((None, K, tn), lambda i, j: (layer, 0, j))],
        out_specs=pl.BlockSpec((tm, tn), lambda i, j: (i, j)),
        out_shape=jax.ShapeDtypeStruct((M, N), F32),
        compiler_params=_cparams("parallel", "arbitrary"),
        name=name,
    )(x, a, w)


def _merge_kernel(oa_ref, ob_ref, oc_ref, g0_ref, g1_ref, g2_ref, wa_ref, wb_ref, wc_ref, o_ref):
    def branch(o_r, g_r, w_r):
        y = jnp.dot(o_r[...], w_r[...].astype(BF16), preferred_element_type=F32)
        return jax.nn.sigmoid(g_r[...]) * y

    m = branch(oa_ref, g0_ref, wa_ref) + branch(ob_ref, g1_ref, wb_ref) + branch(oc_ref, g2_ref, wc_ref)
    o_ref[...] = m.astype(BF16)


def merge_branches(oa, ob, oc, proj, gate_off, wa, wb, wc, layer, name):
    M, K = oa.shape
    N = wa.shape[2]
    tm = _pick(M, (1024, 512, 256, 128))
    tn = _pick(N, (512, 256, 128))
    gb = gate_off // tn
    nb = N // tn
    o_spec = pl.BlockSpec((tm, K), lambda i, j: (i, 0))
    w_spec = pl.BlockSpec((None, K, tn), lambda i, j: (layer, 0, j))
    return pl.pallas_call(
        _merge_kernel,
        grid=(M // tm, nb),
        in_specs=[o_spec, o_spec, o_spec,
                  pl.BlockSpec((tm, tn), lambda i, j: (i, gb + j)),
                  pl.BlockSpec((tm, tn), lambda i, j: (i, gb + nb + j)),
                  pl.BlockSpec((tm, tn), lambda i, j: (i, gb + 2 * nb + j)),
                  w_spec, w_spec, w_spec],
        out_specs=pl.BlockSpec((tm, tn), lambda i, j: (i, j)),
        out_shape=jax.ShapeDtypeStruct((M, N), BF16),
        compiler_params=_cparams("parallel", "arbitrary"),
        name=name,
    )(oa, ob, oc, proj, proj, proj, wa, wb, wc)


LOG2E = 1.4426950408889634


def _strict_upper2(n):
    r = lax.broadcasted_iota(jnp.int32, (2 * n, n), 0)
    c = lax.broadcasted_iota(jnp.int32, (2 * n, n), 1)
    r = jnp.where(r >= n, r - n, r)
    return jnp.where(r > c, 1.0, 0.0).astype(BF16)


def _neg_abs(x):
    bits = pltpu.bitcast(x, jnp.uint32) | jnp.uint32(0x80000000)
    return pltpu.bitcast(bits, F32)


MASKED_LOGIT = -1e30


def _sb_tiles(z2s, v_bfs, u2, crys, chained=False, v_transposed=False):
    staged = []
    for z2 in z2s:
        l2 = jnp.log2(1.0 + jnp.exp2(_neg_abs(z2)))
        log_beta = jnp.minimum(z2, 0.0) - l2
        neg_log_rem = jnp.maximum(z2, 0.0) + l2
        hi = neg_log_rem.astype(BF16)
        if u2.shape[0] == 2 * u2.shape[1]:
            lo = (neg_log_rem - hi.astype(F32)).astype(BF16)
            hi = jnp.concatenate([hi, lo], axis=1)
        suffix = jnp.dot(hi, u2, preferred_element_type=F32)
        staged.append((log_beta, suffix, jnp.sum(neg_log_rem, axis=-1, keepdims=True)))
    contribs, new_crys = [], []
    for n, ((log_beta, suffix, row_sum), v_bf) in enumerate(zip(staged, v_bfs)):
        cry = crys if chained else crys[n]
        w = jnp.exp2(log_beta - suffix - cry).astype(BF16)
        v_contract = 1 if v_transposed else 0
        contribs.append(lax.dot_general(w, v_bf, (((1,), (v_contract,)), ((), ())),
                                        preferred_element_type=F32))
        if chained:
            crys = cry + row_sum
        else:
            new_crys.append(cry + row_sum)
    return contribs, (crys if chained else new_crys)


def _attn_prompt_kernel(bias_ref, q_ref, k_ref, v_ref, o_ref, kb_ref, vb_ref, acc_ref, cry_ref, *,
                        layer, heads_per_blk):
    hp = pl.program_id(1)
    i = pl.program_id(2)
    tq = q_ref.shape[0]
    tk = ATT_TK
    dh = DH_A

    heads = range(heads_per_blk)

    @pl.when(i == 0)
    def _():
        lane = lax.broadcasted_iota(jnp.int32, (k_ref.shape[0], dh), 1)
        for hh in heads:
            b2 = jnp.full(lane.shape, bias_ref[layer * H_A + hp * heads_per_blk + hh] * LOG2E, F32)
            b_hi = b2.astype(BF16).astype(F32)
            extra = jnp.where(lane == 0, b_hi, jnp.where(lane == 1, b2 - b_hi, 0.0)).astype(BF16)
            kb_ref[hh] = jnp.concatenate([k_ref[:, hh * dh:(hh + 1) * dh].astype(BF16), extra], axis=1)
            vb_ref[hh] = v_ref[:, hh * dh:(hh + 1) * dh].astype(BF16)

    u2 = _strict_upper2(tk)[:tk]
    diag_mask = (lax.broadcasted_iota(jnp.int32, (tq, tk), 1)
                 < lax.broadcasted_iota(jnp.int32, (tq, tk), 0))
    ones2 = jnp.where(lax.broadcasted_iota(jnp.int32, (tq, dh), 1) < 2, 1.0, 0.0).astype(BF16)
    q2 = [jnp.concatenate([(q_ref[:, hh * dh:(hh + 1) * dh] * (dh ** -0.5 * LOG2E)).astype(BF16), ones2],
                          axis=1) for hh in heads]

    def tile(j, mask, first):
        rows = pl.ds(pl.multiple_of(j * tk, tk), tk)
        z2s = [lax.dot_general(q2[hh], kb_ref[hh, rows, :], (((1,), (1,)), ((), ())),
                               preferred_element_type=F32) for hh in heads]
        if mask is not None:
            z2s = [jnp.where(mask, z2, MASKED_LOGIT) for z2 in z2s]
        crys = [jnp.zeros((tq, 1), F32) if first else cry_ref[hh] for hh in heads]
        contribs, crys = _sb_tiles(z2s, [vb_ref[hh, rows, :] for hh in heads], u2, crys)
        for hh in heads:
            acc_ref[hh] = contribs[hh] if first else acc_ref[hh] + contribs[hh]
            cry_ref[hh] = crys[hh]

    tile(i, diag_mask, True)

    def body(jj, carry):
        tile(i - 1 - jj, None, False)
        return carry

    lax.fori_loop(0, i, body, 0)
    for hh in heads:
        o_ref[:, hh * dh:(hh + 1) * dh] = acc_ref[hh].astype(o_ref.dtype)


def attn_prompt(proj, sb_bias, layer, B, T, name):
    d_a = H_A * DH_A
    hpb = ATT_HEADS_PER_STEP
    wb = hpb * DH_A
    n_hp = H_A // hpb
    nq = T // ATT_TQ
    kv_blk = d_a // wb
    return pl.pallas_call(
        functools.partial(_attn_prompt_kernel, layer=layer, heads_per_blk=hpb),
        grid=(B, n_hp, nq),
        in_specs=[pl.BlockSpec(memory_space=pltpu.SMEM),
                  pl.BlockSpec((ATT_TQ, wb), lambda b, h, i: (b * nq + i, h)),
                  pl.BlockSpec((T, wb), lambda b, h, i: (b, kv_blk + h)),
                  pl.BlockSpec((T, wb), lambda b, h, i: (b, 2 * kv_blk + h))],
        out_specs=pl.BlockSpec((ATT_TQ, wb), lambda b, h, i: (b * nq + i, h)),
        out_shape=jax.ShapeDtypeStruct((B * T, d_a), BF16),
        scratch_shapes=[pltpu.VMEM((hpb, T, 2 * DH_A), BF16), pltpu.VMEM((hpb, T, DH_A), BF16),
                        pltpu.VMEM((hpb, ATT_TQ, DH_A), F32), pltpu.VMEM((hpb, ATT_TQ, 1), F32)],
        compiler_params=_cparams("parallel", "parallel", "arbitrary"),
        name=name,
    )(sb_bias.reshape(-1), proj, proj, proj)


def _attn_sample_kernel(pt_ref, q_ref, kn_ref, vn_ref, bias_ref, *refs, n_pg):
    del pt_ref
    k_refs = refs[:n_pg]
    v_refs = refs[n_pg:2 * n_pg]
    o_ref = refs[2 * n_pg]
    qbd_ref, acc_ref, cry_ref, kpad_ref, vpad_ref = refs[2 * n_pg + 1:]
    g = pl.program_id(1)
    ts, d_a = q_ref.shape
    rows = ts * H_A
    page = k_refs[0].shape[1]
    head_mask = (lax.broadcasted_iota(jnp.int32, (H_A, d_a), 1) // DH_A
                 == lax.broadcasted_iota(jnp.int32, (H_A, d_a), 0))
    bias2 = bias_ref[...] * LOG2E
    u2 = _strict_upper2(page)

    def logits(kt_ref):
        return jnp.dot(qbd_ref[...], kt_ref[...].astype(BF16), preferred_element_type=F32) + bias2

    @pl.when(g == 0)
    def _():
        q = q_ref[...] * (DH_A ** -0.5 * LOG2E)
        qbd_ref[...] = jnp.concatenate(
            [jnp.where(head_mask, jnp.broadcast_to(q[t:t + 1, :], (H_A, d_a)), 0.0) for t in range(ts)],
            axis=0).astype(BF16)
        kpad_ref[...] = jnp.zeros_like(kpad_ref)
        vpad_ref[...] = jnp.zeros_like(vpad_ref)
        kpad_ref[0:ts, :] = kn_ref[...]
        vpad_ref[0:ts, :] = vn_ref[...]
        z2 = lax.dot_general(qbd_ref[...], kpad_ref[...].astype(BF16), (((1,), (1,)), ((), ())),
                             preferred_element_type=F32) + bias2
        t_of_row = lax.broadcasted_iota(jnp.int32, (rows, page), 0) // H_A
        mask = lax.broadcasted_iota(jnp.int32, (rows, page), 1) < t_of_row
        contribs, crys = _sb_tiles([jnp.where(mask, z2, MASKED_LOGIT)], [vpad_ref[...].astype(BF16)],
                                   u2, [jnp.zeros((rows, 1), F32)])
        acc_ref[...] = contribs[0]
        cry_ref[...] = crys[0]

    contribs, cry = _sb_tiles([logits(k_refs[r]) for r in range(n_pg)],
                              [v_refs[r][...].astype(BF16) for r in range(n_pg)],
                              u2, cry_ref[...], chained=True, v_transposed=True)
    acc = acc_ref[...]
    for contrib in contribs:
        acc = acc + contrib
    acc_ref[...] = acc
    cry_ref[...] = cry

    @pl.when(g == pl.num_programs(1) - 1)
    def _():
        hm = head_mask.astype(F32)
        for t in range(ts):
            o_ref[t:t + 1, :] = jnp.sum(acc_ref[t * H_A:(t + 1) * H_A, :] * hm, axis=0, keepdims=True)


def attn_sample(q, k_new, v_new, cache_kt, cache_vt, page_table, sb_bias, layer, name):
    Bs, Ts, d_a = q.shape
    n_pages = page_table.shape[1]
    page = cache_kt.shape[3]
    n_pg = _pick(n_pages, (16, 8, 4, 2, 1))
    n_grp = n_pages // n_pg
    bias_col = jnp.tile(sb_bias[layer], Ts).reshape(Ts * H_A, 1)

    def page_spec(r):
        return pl.BlockSpec((None, None, d_a, page),
                            lambda b, g, pt: (layer, pt[b, n_pages - 1 - (g * n_pg + r)], 0, 0))

    row_spec = pl.BlockSpec((None, Ts, d_a), lambda b, g, pt: (b, 0, 0))
    grid_spec = pltpu.PrefetchScalarGridSpec(
        num_scalar_prefetch=1,
        grid=(Bs, n_grp),
        in_specs=[row_spec, row_spec, row_spec,
                  pl.BlockSpec((Ts * H_A, 1), lambda b, g, pt: (0, 0))]
                 + [page_spec(r) for r in range(n_pg)] + [page_spec(r) for r in range(n_pg)],
        out_specs=row_spec,
        scratch_shapes=[pltpu.VMEM((Ts * H_A, d_a), BF16),
                        pltpu.VMEM((Ts * H_A, d_a), F32),
                        pltpu.VMEM((Ts * H_A, 1), F32),
                        pltpu.VMEM((page, d_a), F32),
                        pltpu.VMEM((page, d_a), F32)],
    )
    return pl.pallas_call(
        functools.partial(_attn_sample_kernel, n_pg=n_pg),
        grid_spec=grid_spec,
        out_shape=jax.ShapeDtypeStruct((Bs, Ts, d_a), F32),
        compiler_params=_cparams("parallel", "arbitrary"),
        name=name,
    )(page_table, q, k_new, v_new, bias_col, *([cache_kt] * n_pg), *([cache_vt] * n_pg))


def _conv_kernel(a_ref, b_ref, ha_ref, hb_ref, w_ref, cb_ref, lg_ref, lb_ref, y_ref, tail_ref,
                 ubuf, cbuf, sbuf, *, stride, halo, tiles_per_seq, halo_is_state):
    tt = a_ref.shape[0]
    d_b = a_ref.shape[1]
    if halo_is_state:
        ubuf[0:halo, :] = ha_ref[...]
    else:
        first = (pl.program_id(0) % tiles_per_seq) == 0
        uh = ha_ref[...] * jax.nn.sigmoid(hb_ref[...])
        ubuf[0:halo, :] = jnp.where(first, 0.0, uh)
    ubuf[halo:halo + tt, :] = a_ref[...] * jax.nn.sigmoid(b_ref[...])

    offs = [halo - (CONV_W - 1) * stride + j * stride for j in range(CONV_W)]
    groups = {}
    for j, off in enumerate(offs):
        groups.setdefault(off % SUBLANES, []).append(j)
    rc = min(tt, 64)
    for c in range(d_b // LANES):
        cols = slice(c * LANES, (c + 1) * LANES)
        for r, taps in groups.items():
            if r:
                span = max(offs[j] for j in taps) - r + tt
                sbuf[r, 0:span, :] = ubuf[r:r + span, cols]
        for r0 in range(0, tt, rc):
            acc = jnp.zeros((rc, LANES), F32)
            for r, taps in groups.items():
                for j in taps:
                    a8 = offs[j] - r + r0
                    tap = sbuf[r, a8:a8 + rc, :] if r else ubuf[a8:a8 + rc, cols]
                    acc = acc + w_ref[j:j + 1, cols] * tap
            cbuf[r0:r0 + rc, cols] = acc

    ln_rows = min(tt, 32)
    for r0 in range(0, tt, ln_rows):
        cv = cbuf[r0:r0 + ln_rows, :] + cb_ref[...]
        mu = jnp.mean(cv, axis=-1, keepdims=True)
        d = cv - mu
        var = jnp.mean(d * d, axis=-1, keepdims=True)
        y = (d * lax.rsqrt(var + EPS)) * lg_ref[...] + lb_ref[...]
        y_ref[r0:r0 + ln_rows, :] = _silu(y).astype(y_ref.dtype)
    n_tail = tail_ref.shape[0]
    tail_ref[...] = ubuf[halo + tt - n_tail:halo + tt, :]


def conv_module(proj, glu_off, d_b, state, conv_w, conv_b, ln_g, ln_b, layer, seq_len, stride, name):
    M = proj.shape[0]
    ca = glu_off // d_b
    w = conv_w[layer]
    vecs = [conv_b[layer].reshape(1, d_b), ln_g[layer].reshape(1, d_b), ln_b[layer].reshape(1, d_b)]
    vec_spec = pl.BlockSpec((1, d_b), lambda i: (0, 0))
    if state is None:
        tt = _pick(seq_len, (256, 128))
        halo = 32
        tps = seq_len // tt
        n_seq = M // seq_len
        hblk = tt // halo
        halo_specs = [pl.BlockSpec((halo, d_b), lambda i: (jnp.maximum(i * hblk - 1, 0), ca)),
                      pl.BlockSpec((halo, d_b), lambda i: (jnp.maximum(i * hblk - 1, 0), ca + 1))]
        halo_args = [proj, proj]
        n_tail = halo
        tail_shape = (n_seq, n_tail, d_b)
        tail_spec = pl.BlockSpec((None, n_tail, d_b), lambda i: (i // tps, 0, 0))
        grid = (M // tt,)
    else:
        tt = M
        halo = state.shape[0]
        tps = 1
        halo_specs = [pl.BlockSpec((halo, d_b), lambda i: (0, 0)),
                      pl.BlockSpec((halo, d_b), lambda i: (0, 0))]
        halo_args = [state, state]
        n_tail = halo
        tail_shape = (n_tail, d_b)
        tail_spec = pl.BlockSpec((n_tail, d_b), lambda i: (0, 0))
        grid = (1,)
    return pl.pallas_call(
        functools.partial(_conv_kernel, stride=stride, halo=halo, tiles_per_seq=tps,
                          halo_is_state=state is not None),
        grid=grid,
        in_specs=[pl.BlockSpec((tt, d_b), lambda i: (i, ca)),
                  pl.BlockSpec((tt, d_b), lambda i: (i, ca + 1))] + halo_specs
                 + [pl.BlockSpec((CONV_W, d_b), lambda i: (0, 0)), vec_spec, vec_spec, vec_spec],
        out_specs=[pl.BlockSpec((tt, d_b), lambda i: (i, 0)), tail_spec],
        out_shape=[jax.ShapeDtypeStruct((M, d_b), BF16), jax.ShapeDtypeStruct(tail_shape, F32)],
        scratch_shapes=[pltpu.VMEM((halo + tt, d_b), F32), pltpu.VMEM((tt, d_b), F32),
                        pltpu.VMEM((SUBLANES, halo + tt, LANES), F32)],
        compiler_params=_cparams("arbitrary"),
        name=name,
    )(proj, proj, *halo_args, w, *vecs)


def _hgrn_lower_bound(lbraw, layer):
    e = jnp.exp(lbraw - jnp.max(lbraw, axis=0, keepdims=True))
    p = e / jnp.sum(e, axis=0, keepdims=True)
    lb = jnp.zeros((1, lbraw.shape[1]), F32)
    for r in range(1, layer + 1):
        lb = lb + p[r:r + 1, :]
    return lb


def _cumsum_rows(x):
    n = x.shape[0]
    r = lax.broadcasted_iota(jnp.int32, (n, n), 0)
    c = lax.broadcasted_iota(jnp.int32, (n, n), 1)
    tri = jnp.where(r >= c, 1.0, 0.0).astype(BF16)
    hi = x.astype(BF16)
    r1 = x - hi.astype(F32)
    mid = r1.astype(BF16)
    lo = (r1 - mid.astype(F32)).astype(BF16)
    parts = jnp.dot(tri, jnp.concatenate([hi, mid, lo], axis=1), preferred_element_type=F32)
    w = x.shape[1]
    return parts[:, :w] + parts[:, w:2 * w] + parts[:, 2 * w:]


def _hgrn_chunks(heads, *, sub, n_valid):
    c = heads[0][0].shape[0]
    nt = (((1,), (1,)), ((), ()))
    n_sub = -(-min(c, n_valid) // sub)
    row_c = lax.broadcasted_iota(jnp.int32, (c, 1), 0)
    row_l = lax.broadcasted_iota(jnp.int32, (sub, 1), 0)

    kg = []
    for zf, iv, q, gate, lb, gn, st in heads:
        log_sig, log_sig_neg = _log_sigmoid_parts(zf)
        a_ = jnp.log(lb)
        b_ = jnp.log1p(-lb) + log_sig
        log_f = jnp.maximum(a_, b_) + jnp.log(1.0 + jnp.exp(_neg_abs(a_ - b_)))
        k = (1.0 - lb) * jnp.exp(log_sig_neg)
        if n_valid < c:
            valid = lax.broadcasted_iota(jnp.int32, zf.shape, 0) < n_valid
            log_f = jnp.where(valid, log_f, 0.0)
            k = jnp.where(valid, k, 0.0)
        kg.append((k, _cumsum_rows(log_f)))

    o_inter = [lax.dot_general((q * jnp.exp(G)).astype(BF16), st.astype(BF16), nt,
                               preferred_element_type=F32)
               for (zf, iv, q, gate, lb, gn, st), (k, G) in zip(heads, kg)]

    scores = []
    for (zf, iv, q, gate, lb, gn, st), (k, G) in zip(heads, kg):
        per = []
        for I in range(1, n_sub):
            lo = I * sub
            ref = G[lo - 1:lo]
            qd = q[lo:lo + sub] * jnp.exp(G[lo:lo + sub] - ref)
            kd = jnp.where(row_c < lo, k * jnp.exp(jnp.minimum(ref - G, 0.0)), 0.0)
            per.append(lax.dot_general(qd.astype(BF16), kd.astype(BF16), nt, preferred_element_type=F32))
        scores.append(per)

    diag = []
    for (zf, iv, q, gate, lb, gn, st), (k, G) in zip(heads, kg):
        per = []
        for I in range(n_sub):
            lo = I * sub
            qI, GI = q[lo:lo + sub], G[lo:lo + sub]
            o = jnp.zeros((sub, iv.shape[1]), F32)
            for s in range(min(sub, n_valid - lo)):
                sg = lo + s
                e = jnp.exp(jnp.minimum(GI - G[sg:sg + 1], 0.0))
                col = jnp.sum(qI * (k[sg:sg + 1] * e), axis=-1, keepdims=True)
                o = o + jnp.where(row_l >= s, col, 0.0) * iv[sg:sg + 1]
            per.append(o)
        diag.append(per)

    results = []
    for n, ((zf, iv, q, gate, lb, gn, st), (k, G)) in enumerate(zip(heads, kg)):
        v_bf = iv.astype(BF16)
        outs = []
        for I in range(c // sub):
            lo = I * sub
            if I >= n_sub:
                outs.append(jnp.zeros((sub, iv.shape[1]), F32))
                continue
            o = o_inter[n][lo:lo + sub] + diag[n][I]
            if I > 0:
                o = o + jnp.dot(scores[n][I - 1].astype(BF16), v_bf, preferred_element_type=F32)
            outs.append(o)
        o = jnp.concatenate(outs, axis=0) if len(outs) > 1 else outs[0]
        g_end = G[c - 1:c]
        kdec = k * jnp.exp(g_end - G)
        st_new = st * jnp.exp(g_end) + lax.dot_general(v_bf, kdec.astype(BF16), (((0,), (0,)), ((), ())),
                                                      preferred_element_type=F32)
        r = lax.rsqrt(jnp.mean(o * o, axis=-1, keepdims=True) + EPS)
        results.append((((o * r) * gn) * _silu(gate), st_new))
    return results


def _hgrn_prompt_kernel(f_ref, i_ref, q_ref, g_ref, lb_ref, gn_ref, y_ref, s_ref, st_ref, *, layer, nh):
    tt = f_ref.shape[0]
    c = HGRN_CHUNK
    n = DK_C
    ti = pl.program_id(2)
    lbs = _hgrn_lower_bound(lb_ref[...], layer)
    gns = gn_ref[...]

    @pl.when(ti == 0)
    def _():
        st_ref[...] = jnp.zeros_like(st_ref)

    def body(ci, carry):
        rows = pl.ds(pl.multiple_of(ci * c, c), c)
        heads = []
        for h in range(nh):
            cols = slice(h * n, (h + 1) * n)
            heads.append((f_ref[rows, cols], i_ref[rows, cols], q_ref[rows, cols], g_ref[rows, cols],
                          lbs[:, cols], gns[:, cols], st_ref[h]))
        for h, (y, st_new) in enumerate(_hgrn_chunks(heads, sub=HGRN_SUB, n_valid=c)):
            y_ref[rows, h * n:(h + 1) * n] = y.astype(y_ref.dtype)
            st_ref[h] = st_new
        return carry

    lax.fori_loop(0, tt // c, body, 0)

    @pl.when(ti == pl.num_programs(2) - 1)
    def _():
        for h in range(nh):
            s_ref[h] = st_ref[h].T


def hgrn_prompt(proj, f_off, lb_raw, norm_g, layer, B, T, name):
    d_c = H_C * DK_C
    nh = HGRN_HEADS_PER_STEP
    wb = nh * DK_C
    n_hg = H_C // nh
    tt = _pick(T, (1024, 512, 256, 128, 64))
    nt = T // tt
    fb = f_off // wb

    def col(k):
        return pl.BlockSpec((tt, wb), lambda b, h, t: (b * nt + t, fb + k * n_hg + h))

    return pl.pallas_call(
        functools.partial(_hgrn_prompt_kernel, layer=layer, nh=nh),
        grid=(B, n_hg, nt),
        in_specs=[col(0), col(1), col(2), col(3),
                  pl.BlockSpec((lb_raw.shape[0], wb), lambda b, h, t: (0, h)),
                  pl.BlockSpec((1, wb), lambda b, h, t: (0, h))],
        out_specs=[pl.BlockSpec((tt, wb), lambda b, h, t: (b * nt + t, h)),
                   pl.BlockSpec((None, nh, DK_C, DK_C), lambda b, h, t: (b, h, 0, 0))],
        out_shape=[jax.ShapeDtypeStruct((B * T, d_c), BF16),
                   jax.ShapeDtypeStruct((B, H_C, DK_C, DK_C), F32)],
        scratch_shapes=[pltpu.VMEM((nh, DK_C, DK_C), F32)],
        compiler_params=_cparams("parallel", "parallel", "arbitrary"),
        name=name,
    )(proj, proj, proj, proj, lb_raw, norm_g[layer].reshape(1, d_c))


def _hgrn_sample_kernel(f_ref, i_ref, q_ref, g_ref, lb_ref, gn_ref, s0_ref, y_ref, s_ref, *, layer, ts, nh):
    n = DK_C
    lbs = _hgrn_lower_bound(lb_ref[...], layer)
    gns = gn_ref[...]
    heads = []
    for h in range(nh):
        cols = slice(h * n, (h + 1) * n)
        heads.append((f_ref[:, cols], i_ref[:, cols], q_ref[:, cols], g_ref[:, cols],
                      lbs[:, cols], gns[:, cols], s0_ref[h].T))
    for h, (y, st_new) in enumerate(_hgrn_chunks(heads, sub=f_ref.shape[0], n_valid=ts)):
        y_ref[:, h * n:(h + 1) * n] = y.astype(y_ref.dtype)
        s_ref[h] = st_new.T


def hgrn_sample(fiqg, state, lb_raw, norm_g, layer, ts, name):
    _, Bs, tp, d_c = fiqg.shape
    nh = H_C
    wb = nh * DK_C

    def col(k):
        return pl.BlockSpec((None, None, tp, wb), lambda b, h: (k, b, 0, h))

    st_spec = pl.BlockSpec((None, nh, DK_C, DK_C), lambda b, h: (b, h, 0, 0))
    return pl.pallas_call(
        functools.partial(_hgrn_sample_kernel, layer=layer, ts=ts, nh=nh),
        grid=(Bs, H_C // nh),
        in_specs=[col(0), col(1), col(2), col(3),
                  pl.BlockSpec((lb_raw.shape[0], wb), lambda b, h: (0, h)),
                  pl.BlockSpec((1, wb), lambda b, h: (0, h)),
                  st_spec],
        out_specs=[pl.BlockSpec((None, tp, wb), lambda b, h: (b, 0, h)), st_spec],
        out_shape=[jax.ShapeDtypeStruct((Bs, tp, d_c), BF16),
                   jax.ShapeDtypeStruct(state.shape, F32)],
        compiler_params=_cparams("parallel", "parallel"),
        name=name,
    )(fiqg, fiqg, fiqg, fiqg, lb_raw, norm_g[layer].reshape(1, d_c), state)


def _ffn_act_kernel(g_ref, u_ref, h_ref, w_ref, o_ref, gbuf, *, stride):
    tt = g_ref.shape[0]
    halo = h_ref.shape[0]
    gbuf[0:halo, :] = h_ref[...]
    gbuf[halo:halo + tt, :] = g_ref[...]
    a = w_ref[FFN_CONV_W - 1:FFN_CONV_W, :] * g_ref[...]
    for j in range(FFN_CONV_W - 1):
        s = halo - (FFN_CONV_W - 1 - j) * stride
        a = a + w_ref[j:j + 1, :] * gbuf[s:s + tt, :]
    o_ref[...] = (_silu(a) * u_ref[...]).astype(o_ref.dtype)


def _ffn_in_act_kernel(x_ref, g_ref, wg_ref, wu_ref, cw_ref, act_ref, tail_ref, h_ref, gbuf, carry_ref, *,
                       tiles_per_seq):
    i = pl.program_id(0)
    j = pl.program_id(1)
    tm = act_ref.shape[0]
    halo = tail_ref.shape[0]

    @pl.when(j == 0)
    def _():
        xf = x_ref[...]
        r = lax.rsqrt(jnp.mean(xf * xf, axis=-1, keepdims=True) + EPS)
        h_ref[...] = ((xf * r) * g_ref[...]).astype(BF16)

    h = h_ref[...]
    gate = jnp.dot(h, wg_ref[...].astype(BF16), preferred_element_type=F32)
    up = jnp.dot(h, wu_ref[...].astype(BF16), preferred_element_type=F32)
    first = (i % tiles_per_seq) == 0
    gbuf[0:halo, :] = jnp.where(first, 0.0, carry_ref[j])
    gbuf[halo:halo + tm, :] = gate
    a = cw_ref[FFN_CONV_W - 1:FFN_CONV_W, :] * gate
    for k in range(FFN_CONV_W - 1):
        s = halo - (FFN_CONV_W - 1 - k)
        a = a + cw_ref[k:k + 1, :] * gbuf[s:s + tm, :]
    act_ref[...] = (_silu(a) * up).astype(act_ref.dtype)
    last = gbuf[tm:tm + halo, :]
    carry_ref[j] = last
    tail_ref[...] = last


def ffn_in_act(x, g, w_gate, w_up, conv_w, layer, seq_len, name):
    M, D = x.shape
    N = w_gate.shape[2]
    tm = _pick(seq_len, (1024, 512, 256, 128))
    tn = _pick(N, (512, 256, 128))
    tps = seq_len // tm
    halo = SUBLANES
    g3 = g.reshape(g.shape[0], 1, D)
    w_spec = pl.BlockSpec((None, D, tn), lambda i, j: (layer, 0, j))
    return pl.pallas_call(
        functools.partial(_ffn_in_act_kernel, tiles_per_seq=tps),
        grid=(M // tm, N // tn),
        in_specs=[pl.BlockSpec((tm, D), lambda i, j: (i, 0)),
                  pl.BlockSpec((None, 1, D), lambda i, j: (layer, 0, 0)),
                  w_spec, w_spec,
                  pl.BlockSpec((None, FFN_CONV_W, tn), lambda i, j: (layer, 0, j))],
        out_specs=[pl.BlockSpec((tm, tn), lambda i, j: (i, j)),
                   pl.BlockSpec((None, halo, tn), lambda i, j: (i, 0, j))],
        out_shape=[jax.ShapeDtypeStruct((M, N), BF16),
                   jax.ShapeDtypeStruct((M // tm, halo, N), F32)],
        scratch_shapes=[pltpu.VMEM((tm, D), BF16), pltpu.VMEM((halo + tm, tn), F32),
                        pltpu.VMEM((N // tn, halo, tn), F32)],
        compiler_params=_cparams("arbitrary", "arbitrary"),
        name=name,
    )(x, g3, w_gate, w_up, conv_w)


def ffn_act(G, U, state, conv_w, layer, stride, name):
    M, d_ff = G.shape
    tc = _pick(d_ff, (1408, 512, 256, 128))
    halo = state.shape[0]
    return pl.pallas_call(
        functools.partial(_ffn_act_kernel, stride=stride),
        grid=(d_ff // tc,),
        in_specs=[pl.BlockSpec((M, tc), lambda c: (0, c)),
                  pl.BlockSpec((M, tc), lambda c: (0, c)),
                  pl.BlockSpec((halo, tc), lambda c: (0, c)),
                  pl.BlockSpec((None, FFN_CONV_W, tc), lambda c: (layer, 0, c))],
        out_specs=pl.BlockSpec((M, tc), lambda c: (0, c)),
        out_shape=jax.ShapeDtypeStruct((M, d_ff), BF16),
        scratch_shapes=[pltpu.VMEM((halo + M, tc), F32)],
        compiler_params=_cparams("parallel"),
        name=name,
    )(G, U, state, conv_w)


def kernel(x_prompt, x_sample, cache_k, cache_v, state_conv, state_hgrn, state_ffn_conv, page_table, hgrn_lower_bound, sb_bias, norm_mix_g, w_in, conv_w, conv_b, conv_ln_g, conv_ln_b, hgrn_norm_g, w_out_a, w_out_b, w_out_c, w_o, norm_ffn_g, w_gate, w_up, ffn_conv_w, w_down, norm_final_g):
    Bp, T, D = x_prompt.shape
    Bs, Ts, _ = x_sample.shape
    depth = w_in.shape[0]
    d_a = H_A * DH_A
    d_b = conv_w.shape[2]
    d_c = H_C * DK_C
    d_ff = w_gate.shape[2]
    off_glu = 3 * d_a
    off_f = off_glu + 2 * d_b
    off_gates = off_f + 4 * d_c
    tp = 16

    n_phys, page = cache_k.shape[1], cache_k.shape[2]
    ckt = cache_k.transpose(0, 1, 3, 4, 2).reshape(depth, n_phys, d_a, page)
    cvt = cache_v.transpose(0, 1, 3, 4, 2).reshape(depth, n_phys, d_a, page)
    xp = x_prompt.reshape(Bp * T, D)
    xs = x_sample.transpose(1, 0, 2).reshape(Ts * Bs, D)

    def to_batch_major(a):
        return a.reshape(Ts, Bs, a.shape[-1]).transpose(1, 0, 2)

    def to_time_major(a):
        return a.transpose(1, 0, 2).reshape(Ts * Bs, a.shape[-1])

    w_in, w_out_a, w_out_b, w_out_c, w_o, w_gate, w_up, w_down = [
        cast_bf16(w, f"cast_{n}") for n, w in enumerate(
            (w_in, w_out_a, w_out_b, w_out_c, w_o, w_gate, w_up, w_down))]

    outs = {k: [] for k in ("kp", "vp", "ks", "vs", "cp", "cs", "hp", "hs", "fp", "fs")}
    for l in range(depth):
        tag = f"l{l}"
        (pp,) = norm_matmul(xp, norm_mix_g, [w_in], l, f"in_p_{tag}")
        oa = attn_prompt(pp, sb_bias, l, Bp, T, f"attn_p_{tag}")
        ob, tail = conv_module(pp, off_glu, d_b, None, conv_w, conv_b, conv_ln_g, conv_ln_b, l, T, 1,
                               f"conv_p_{tag}")
        oc, hst = hgrn_prompt(pp, off_f, hgrn_lower_bound, hgrn_norm_g, l, Bp, T, f"hgrn_p_{tag}")
        mg = merge_branches(oa, ob, oc, pp, off_gates, w_out_a, w_out_b, w_out_c, l, f"merge_p_{tag}")
        xp = matmul_residual(xp, mg, w_o, l, f"wo_p_{tag}")
        act, gtail = ffn_in_act(xp, norm_ffn_g, w_gate, w_up, ffn_conv_w, l, T, f"ffn_in_p_{tag}")
        xp = matmul_residual(xp, act, w_down, l, f"down_p_{tag}")
        outs["kp"].append(transpose_cols(pp, d_a, d_a, Bp, T, f"kt_p_{tag}").reshape(Bp, H_A, DH_A, T))
        outs["vp"].append(transpose_cols(pp, 2 * d_a, d_a, Bp, T, f"vt_p_{tag}").reshape(Bp, H_A, DH_A, T))
        outs["cp"].append(tail[:, tail.shape[1] - (CONV_W - 1):, :])
        outs["hp"].append(hst)
        gtail = gtail.reshape(Bp, gtail.shape[0] // Bp, gtail.shape[1], d_ff)[:, -1]
        outs["fp"].append(gtail[:, gtail.shape[1] - (FFN_CONV_W - 1):, :])

        (ps,) = norm_matmul(xs, norm_mix_g, [w_in], l, f"in_s_{tag}")
        qkv = to_batch_major(ps[:, :3 * d_a])
        q_s, k_s, v_s = qkv[..., :d_a], qkv[..., d_a:2 * d_a], qkv[..., 2 * d_a:]
        oa = attn_sample(q_s, k_s, v_s, ckt, cvt, page_table, sb_bias, l, f"attn_s_{tag}")
        oa = to_time_major(oa).astype(BF16)
        cst =state_conv[l].transpose(1, 0, 2).reshape((CONV_W - 1) * Bs, d_b)
        ob, tail = conv_module(ps, off_glu, d_b, cst, conv_w, conv_b, conv_ln_g, conv_ln_b, l, Ts, Bs,
                               f"conv_s_{tag}")
        fiqg = to_batch_major(ps[:, off_f:off_f + 4 * d_c]).reshape(Bs, Ts, 4, d_c)
        fiqg = jnp.pad(fiqg.transpose(2, 0, 1, 3), ((0, 0), (0, 0), (0, tp - Ts), (0, 0)))
        oc, hst = hgrn_sample(fiqg, state_hgrn[l], hgrn_lower_bound, hgrn_norm_g, l, Ts, f"hgrn_s_{tag}")
        oc = to_time_major(oc[:, :Ts, :])
        mg = merge_branches(oa, ob, oc, ps, off_gates, w_out_a, w_out_b, w_out_c, l, f"merge_s_{tag}")
        xs = matmul_residual(xs, mg, w_o, l, f"wo_s_{tag}")
        gs, us = norm_matmul(xs, norm_ffn_g, [w_gate, w_up], l, f"ffn_in_s_{tag}")
        fst = state_ffn_conv[l].transpose(1, 0, 2).reshape((FFN_CONV_W - 1) * Bs, d_ff)
        act = ffn_act(gs, us, fst, ffn_conv_w, l, Bs, f"ffn_act_s_{tag}")
        xs = matmul_residual(xs, act, w_down, l, f"down_s_{tag}")
        outs["ks"].append(k_s.reshape(Bs, Ts, H_A, DH_A))
        outs["vs"].append(v_s.reshape(Bs, Ts, H_A, DH_A))
        outs["cs"].append(tail.reshape(CONV_W - 1, Bs, d_b).transpose(1, 0, 2))
        outs["hs"].append(hst)
        ext_g = jnp.concatenate([fst, gs], axis=0)
        outs["fs"].append(ext_g[ext_g.shape[0] - (FFN_CONV_W - 1) * Bs:]
                          .reshape(FFN_CONV_W - 1, Bs, d_ff).transpose(1, 0, 2))

    y_p = rmsnorm_rows(xp, norm_final_g, "final_p").reshape(Bp, T, D)
    y_s = to_batch_major(rmsnorm_rows(xs, norm_final_g, "final_s"))
    st = lambda k: jnp.stack(outs[k])
    kv_p = lambda k: st(k).transpose(0, 1, 4, 2, 3)
    return (y_p, y_s, kv_p("kp"), kv_p("vp"), st("ks"), st("vs"), st("cp"), st("cs"),
            st("hp"), st("hs"), st("fp"), st("fs"))
```
